```python
import jax, jax.numpy as jnp
from jax import lax
import numpy as np

D_MODEL = 4096
BATCH = 1
SEQ = 8192
DEPTH = 1
DEC_BATCH = 32
DEC_SEQ = 8
PAST_LEN = 8192
PAGE_SIZE = 128

HEAD_DIM = 128
N_HEADS = D_MODEL // HEAD_DIM
H_HGRN = N_HEADS // 2
H_ATT = N_HEADS - H_HGRN
DK = HEAD_DIM
DV = HEAD_DIM
W_HGRN = H_HGRN * DV
W_ATT = H_ATT * HEAD_DIM
MOBA_BLOCK = 256
MOBA_TOPK = 3
Q_CHUNK = 32
HGRN_CHUNK = 64
D_FF = -(-8 * D_MODEL // (3 * 256)) * 256
LAYER_IDX = 0
N_LB = DEPTH + 1
EPS = 1e-6
ATT_SCALE = HEAD_DIM ** -0.5
SPLIT_SIZES = (H_HGRN * DK, H_HGRN * DK, W_HGRN, W_HGRN, W_ATT, W_ATT, W_ATT)
D_IN = sum(SPLIT_SIZES)
SPLIT_POINTS = [int(s) for s in np.cumsum(SPLIT_SIZES)[:-1]]

kernel_name = "hymba_hgrn2_moba_decoder_step"


def rmsnorm(x, g):
    xf = x.astype(jnp.float32)
    y = xf * lax.rsqrt(jnp.mean(xf * xf, axis=-1, keepdims=True) + EPS)
    return (y * g.astype(jnp.float32)).astype(x.dtype)


def alibi_slopes(n):
    return 2.0 ** (-8.0 * jnp.arange(1, n + 1, dtype=jnp.float32) / n)


def mixer_inputs(h, w_in, lb):
    B, T, _ = h.shape
    hq, hf, hi, hg, aq, ak, av = jnp.split(jnp.einsum('btd,de->bte', h, w_in), SPLIT_POINTS, axis=-1)
    f = lb + (1.0 - lb) * jax.nn.sigmoid(hf.astype(jnp.float32))
    q = jax.nn.silu(hq).reshape(B, T, H_HGRN, DK)
    k = (1.0 - f).reshape(B, T, H_HGRN, DK)
    logf = jnp.log(f).reshape(B, T, H_HGRN, DK)
    v = hi.reshape(B, T, H_HGRN, DV)
    shp = (B, T, H_ATT, HEAD_DIM)
    return (q, k, v, logf), hg, aq.reshape(shp) * ATT_SCALE, ak.reshape(shp), av.reshape(shp)


def hgrn2_scan(q, k, v, logf, s0):
    B, T = q.shape[:2]
    c = min(HGRN_CHUNK, T)
    pad = (-T) % c
    f32 = jnp.float32
    q, k, v, logf = (a.astype(f32) for a in (q, k, v, logf))
    if pad:
        pw = ((0, 0), (0, pad), (0, 0), (0, 0))
        q, k, v, logf = (jnp.pad(a, pw) for a in (q, k, v, logf))
    n = (T + pad) // c

    def to_chunks(a):
        return a.reshape(B, n, c, *a.shape[2:]).swapaxes(0, 1)

    causal = jnp.tril(jnp.ones((c, c), dtype=bool))[None, :, :, None, None]

    def step(S, inp):
        qc, kc, vc, gc = inp
        b = jnp.cumsum(gc, axis=1)
        diff = b[:, :, None] - b[:, None, :]
        decay = jnp.exp(jnp.where(causal, diff, -jnp.inf))
        a = jnp.einsum('bthd,btshd,bshd->bhts', qc, decay, kc)
        o = jnp.einsum('bhts,bshv->bthv', a, vc) + jnp.einsum('bthd,bhdv->bthv', qc * jnp.exp(b), S)
        b_last = b[:, -1]
        k_dec = kc * jnp.exp(b_last[:, None] - b)
        S = jnp.exp(b_last)[..., None] * S + jnp.einsum('bshd,bshv->bhdv', k_dec, vc)
        return S, o

    S, o = lax.scan(step, s0.astype(f32), (to_chunks(q), to_chunks(k), to_chunks(v), to_chunks(logf)))
    o = o.swapaxes(0, 1).reshape(B, n * c, H_HGRN, DV)[:, :T]
    return o, S.astype(s0.dtype)


def moba_core(q, q_pos, k_own, v_own, own_pos, slopes, sel=None):
    f32 = jnp.float32
    B, T, H, d = q.shape
    qf = q.astype(f32)
    s_own = jnp.einsum('bthd,brhd->bthr', qf, k_own.astype(f32))
    dist_own = q_pos[:, None] - own_pos[None, :]
    s_own = s_own - slopes[None, None, :, None] * dist_own[None, :, None, :]
    s_own = jnp.where((dist_own >= 0)[None, :, None, :], s_own, -jnp.inf)
    if sel is None:
        p = jax.nn.softmax(s_own, axis=-1)
        out = jnp.einsum('bthr,brhd->bthd', p, v_own.astype(f32))
        return out.astype(q.dtype)
    k_sel, v_sel, sel_pos, sel_valid = sel
    K, S = k_sel.shape[3], k_sel.shape[4]
    s_sel = jnp.einsum('bthd,bthksd->bthks', qf, k_sel.astype(f32))
    s_sel = s_sel - slopes[None, None, :, None, None] * (q_pos[None, :, None, None, None] - sel_pos)
    if sel_valid is not None:
        s_sel = jnp.where(sel_valid[..., None], s_sel, -jnp.inf)
    p = jax.nn.softmax(jnp.concatenate([s_sel.reshape(B, T, H, K * S), s_own], axis=-1), axis=-1)
    p_sel = p[..., :K * S].reshape(B, T, H, K, S)
    p_own = p[..., K * S:]
    out = (jnp.einsum('bthks,bthksd->bthd', p_sel, v_sel.astype(f32))
           + jnp.einsum('bthr,brhd->bthd', p_own, v_own.astype(f32)))
    return out.astype(q.dtype)


def moba_prompt(q, k, v, slopes):
    B, T, H, d = q.shape
    nb = -(-T // MOBA_BLOCK)
    pw = ((0, 0), (0, nb * MOBA_BLOCK - T), (0, 0), (0, 0))
    kp, vp = jnp.pad(k, pw), jnp.pad(v, pw)
    kb = kp.reshape(B, nb, MOBA_BLOCK, H, d)
    vb = vp.reshape(B, nb, MOBA_BLOCK, H, d)
    topk = min(MOBA_TOPK, nb - 1)
    q_blk = jnp.arange(T) // MOBA_BLOCK
    if topk > 0:
        k_mean = jnp.mean(kb.astype(jnp.float32), axis=2)
        gate = jnp.einsum('bthd,bjhd->bthj', q.astype(jnp.float32), k_mean)
        past = jnp.arange(nb)[None, None, None, :] < q_blk[None, :, None, None]
        _, sel = lax.top_k(jnp.where(past, gate, -jnp.inf), topk)
        sel_valid = sel < q_blk[None, :, None, None]
    bidx = jnp.arange(B)[:, None, None, None]
    hidx = jnp.arange(H)[None, None, :, None]
    blk_rows = jnp.arange(MOBA_BLOCK)

    def chunk_fn(ci):
        t0 = ci * Q_CHUNK
        blk = t0 // MOBA_BLOCK
        qc = lax.dynamic_slice_in_dim(q, t0, Q_CHUNK, axis=1)
        q_pos = (t0 + jnp.arange(Q_CHUNK)).astype(jnp.float32)
        k_own = lax.dynamic_slice_in_dim(kp, blk * MOBA_BLOCK, MOBA_BLOCK, axis=1)
        v_own = lax.dynamic_slice_in_dim(vp, blk * MOBA_BLOCK, MOBA_BLOCK, axis=1)
        own_pos = (blk * MOBA_BLOCK + blk_rows).astype(jnp.float32)
        sel_args = None
        if topk > 0:
            selc = lax.dynamic_slice_in_dim(sel, t0, Q_CHUNK, axis=1)
            validc = lax.dynamic_slice_in_dim(sel_valid, t0, Q_CHUNK, axis=1)
            k_sel = kb[bidx, selc, :, hidx]
            v_sel = vb[bidx, selc, :, hidx]
            sel_pos = (selc[..., None] * MOBA_BLOCK + blk_rows).astype(jnp.float32)
            sel_args = (k_sel, v_sel, sel_pos, validc)
        return moba_core(qc, q_pos, k_own, v_own, own_pos, slopes, sel_args)

    out = lax.map(chunk_fn, jnp.arange(T // Q_CHUNK))
    return jnp.moveaxis(out, 0, 1).reshape(B, T, H, d)


def moba_sample(q, k_new, v_new, cache_k, cache_v, page_table, slopes):
    T = q.shape[1]
    ppb = MOBA_BLOCK // PAGE_SIZE
    cur_blk = PAST_LEN // MOBA_BLOCK
    n_tail_pages = PAST_LEN // PAGE_SIZE - cur_blk * ppb
    n_tail = n_tail_pages * PAGE_SIZE
    topk = min(MOBA_TOPK, cur_blk)
    q_pos = (PAST_LEN + jnp.arange(T)).astype(jnp.float32)
    own_pos = (cur_blk * MOBA_BLOCK + jnp.arange(n_tail + T)).astype(jnp.float32)
    hidx = jnp.arange(H_ATT)[None, :, None]
    blk_rows = jnp.arange(MOBA_BLOCK)

    def one_seq(args):
        qb, kb_new, vb_new, pt = args
        tail = pt[cur_blk * ppb:]
        k_own = jnp.concatenate([cache_k[tail].reshape(n_tail, H_ATT, HEAD_DIM), kb_new], axis=0)
        v_own = jnp.concatenate([cache_v[tail].reshape(n_tail, H_ATT, HEAD_DIM), vb_new], axis=0)
        sel_args = None
        if topk > 0:
            k_past = cache_k[pt[:cur_blk * ppb]].reshape(cur_blk, MOBA_BLOCK, H_ATT, HEAD_DIM)
            k_mean = jnp.mean(k_past.astype(jnp.float32), axis=1)
            gate = jnp.einsum('thd,jhd->thj', qb.astype(jnp.float32), k_mean)
            _, sel = lax.top_k(gate, topk)
            k_sel = k_past[sel, :, hidx]
            phys = pt[sel[..., None] * ppb + jnp.arange(ppb)]
            v_sel = cache_v[phys, :, hidx[..., None]].reshape(T, H_ATT, topk, MOBA_BLOCK, HEAD_DIM)
            sel_pos = (sel[..., None] * MOBA_BLOCK + blk_rows).astype(jnp.float32)
            sel_args = (k_sel[None], v_sel[None], sel_pos[None], None)
        out = moba_core(qb[None], q_pos, k_own[None], v_own[None], own_pos, slopes, sel_args)
        return out[0]

    return lax.map(one_seq, (q, k_new, v_new, page_table))


def mixer_output(x, o_hgrn, gate, o_att, hgrn_out_norm, attn_out_norm, w_out, norm_post_mix,
                 norm_pre_ffn, w_gate, w_up, w_down, norm_post_ffn):
    B, T, _ = x.shape
    oh = rmsnorm(o_hgrn.astype(x.dtype), hgrn_out_norm).reshape(B, T, W_HGRN) * jax.nn.silu(gate)
    oa = rmsnorm(o_att.reshape(B, T, W_ATT), attn_out_norm)
    mix = jnp.einsum('bte,ed->btd', jnp.concatenate([oh, oa], axis=-1), w_out)
    x = x + rmsnorm(mix, norm_post_mix)
    h = rmsnorm(x, norm_pre_ffn)
    ff = jnp.einsum('btf,fd->btd', jax.nn.silu(jnp.einsum('btd,df->btf', h, w_gate))
                    * jnp.einsum('btd,df->btf', h, w_up), w_down)
    return x + rmsnorm(ff, norm_post_ffn)


def setup_inputs(seed: int = 0) -> dict:
    key = jax.random.key(seed)
    ks = jax.random.split(key, 20)
    f32 = jnp.float32
    n_pages = PAST_LEN // PAGE_SIZE
    n_pool = (DEC_BATCH * n_pages * 5) // 4
    nrm = jax.random.normal
    page_table = jax.random.permutation(ks[4], n_pool)[:DEC_BATCH * n_pages]
    return {
        "x_prompt": nrm(ks[0], (BATCH, SEQ, D_MODEL), f32),
        "x_sample": nrm(ks[1], (DEC_BATCH, DEC_SEQ, D_MODEL), f32),
        "cache_k": nrm(ks[2], (n_pool, PAGE_SIZE, H_ATT, HEAD_DIM), f32),
        "cache_v": nrm(ks[3], (n_pool, PAGE_SIZE, H_ATT, HEAD_DIM), f32),
        "page_table": page_table.reshape(DEC_BATCH, n_pages).astype(jnp.int32),
        "state_hgrn": 0.5 * nrm(ks[5], (DEC_BATCH, H_HGRN, DK, DV), f32),
        "norm_pre_mix": 1.0 + 0.05 * nrm(ks[6], (D_MODEL,), f32),
        "w_in": nrm(ks[7], (D_MODEL, D_IN), f32) * D_MODEL ** -0.5,
        "hgrn_lb_logits": nrm(ks[8], (N_LB, H_HGRN * DK), f32),
        "hgrn_out_norm": 1.0 + 0.05 * nrm(ks[9], (DV,), f32),
        "attn_out_norm": 1.0 + 0.05 * nrm(ks[10], (W_ATT,), f32),
        "w_out": nrm(ks[11], (W_HGRN + W_ATT, D_MODEL), f32) * (W_HGRN + W_ATT) ** -0.5,
        "norm_post_mix": 1.0 + 0.05 * nrm(ks[12], (D_MODEL,), f32),
        "norm_pre_ffn": 1.0 + 0.05 * nrm(ks[13], (D_MODEL,), f32),
        "w_gate": nrm(ks[14], (D_MODEL, D_FF), f32) * D_MODEL ** -0.5,
        "w_up": nrm(ks[15], (D_MODEL, D_FF), f32) * D_MODEL ** -0.5,
        "w_down": nrm(ks[16], (D_FF, D_MODEL), f32) * D_FF ** -0.5,
        "norm_post_ffn": 1.0 + 0.05 * nrm(ks[17], (D_MODEL,), f32),
    }


def reference(x_prompt, x_sample, cache_k, cache_v, page_table, state_hgrn, norm_pre_mix, w_in,
              hgrn_lb_logits, hgrn_out_norm, attn_out_norm, w_out, norm_post_mix, norm_pre_ffn,
              w_gate, w_up, w_down, norm_post_ffn):
    slopes = alibi_slopes(H_ATT)
    lb = jnp.cumsum(jax.nn.softmax(hgrn_lb_logits.astype(jnp.float32), axis=0), axis=0)[LAYER_IDX]
    out_args = (hgrn_out_norm, attn_out_norm, w_out, norm_post_mix, norm_pre_ffn,
                w_gate, w_up, w_down, norm_post_ffn)

    hgrn_p, gate_p, aq_p, k_prompt, v_prompt = mixer_inputs(rmsnorm(x_prompt, norm_pre_mix), w_in, lb)
    s0 = jnp.zeros((x_prompt.shape[0], H_HGRN, DK, DV), state_hgrn.dtype)
    o_hgrn_p, state_prompt = hgrn2_scan(*hgrn_p, s0)
    o_att_p = moba_prompt(aq_p, k_prompt, v_prompt, slopes)
    y_prompt = mixer_output(x_prompt, o_hgrn_p, gate_p, o_att_p, *out_args)

    hgrn_s, gate_s, aq_s, k_sample, v_sample = mixer_inputs(rmsnorm(x_sample, norm_pre_mix), w_in, lb)
    o_hgrn_s, state_sample = hgrn2_scan(*hgrn_s, state_hgrn)
    o_att_s = moba_sample(aq_s, k_sample, v_sample, cache_k, cache_v, page_table, slopes)
    y_sample = mixer_output(x_sample, o_hgrn_s, gate_s, o_att_s, *out_args)

    return (y_prompt, y_sample, k_prompt, v_prompt, k_sample, v_sample, state_prompt, state_sample)
```

```python
import functools

import jax
import jax.numpy as jnp
from jax import lax
from jax.experimental import pallas as pl
from jax.experimental.pallas import tpu as pltpu

F32 = jnp.float32
BF16 = jnp.bfloat16

EPS = 1e-6
HEAD_DIM = 128
N_HGRN_HEADS = 16
N_ATT_HEADS = 16
W_GROUP = N_HGRN_HEADS * HEAD_DIM
MOBA_BLOCK = 256
MOBA_TOPK = 3
PAGE_SIZE = 128
PAGES_PER_BLOCK = MOBA_BLOCK // PAGE_SIZE
ATT_SCALE = HEAD_DIM ** -0.5
HGRN_SUB = 16
NEG_INF = float("-inf")
VMEM_LIMIT = 56 * 1024 * 1024

ZQ, ZF, ZI, ZG, ZA = 0, 16, 32, 48, 64


def _cparams(sem):
    return pltpu.CompilerParams(dimension_semantics=sem, vmem_limit_bytes=VMEM_LIMIT)


def _nt(a, b, precision=None):
    return lax.dot_general(a, b, (((1,), (1,)), ((), ())), preferred_element_type=F32,
                           precision=precision)


def _sigmoid(x):
    return 1.0 / (1.0 + jnp.exp(-x))


def _norm_kernel(x_ref, g_ref, o_ref):
    x = x_ref[...]
    y = x * lax.rsqrt(jnp.mean(x * x, axis=-1, keepdims=True) + EPS)
    o_ref[...] = (y * g_ref[...]).astype(BF16)


def rmsnorm_bf16(x, g, tm):
    m, d = x.shape
    return pl.pallas_call(
        _norm_kernel,
        out_shape=jax.ShapeDtypeStruct((m, d), BF16),
        grid=(m // tm,),
        in_specs=[pl.BlockSpec((tm, d), lambda i: (i, 0)), pl.BlockSpec((1, d), lambda i: (0, 0))],
        out_specs=pl.BlockSpec((tm, d), lambda i: (i, 0)),
        compiler_params=_cparams(("arbitrary",)),
        name="rmsnorm_bf16",
    )(x, g.reshape(1, d))


def _mm_kernel(a_ref, w_ref, o_ref):
    o_ref[...] = jnp.dot(a_ref[...], w_ref[...], preferred_element_type=F32)


def matmul_cols(a, w, col0, n, tm, tn):
    m, k = a.shape
    cb = col0 // tn
    return pl.pallas_call(
        _mm_kernel,
        out_shape=jax.ShapeDtypeStruct((m, n), F32),
        grid=(m // tm, n // tn),
        in_specs=[pl.BlockSpec((tm, k), lambda i, j: (i, 0)),
                  pl.BlockSpec((k, tn), lambda i, j: (0, cb + j))],
        out_specs=pl.BlockSpec((tm, tn), lambda i, j: (i, j)),
        compiler_params=_cparams(("arbitrary", "arbitrary")),
        name="matmul_cols",
    )(a, w)


def _cumsum_rows(g):
    c = g.shape[0]
    row = lax.broadcasted_iota(jnp.int32, g.shape, 0)
    b = g
    sh = 1
    while sh < c:
        b = b + jnp.where(row >= sh, pltpu.roll(b, sh, 0), 0.0)
        sh *= 2
    return b


def _hgrn_kernel(q_ref, f_ref, i_ref, lbl_ref, s0_ref, o_ref, sout_ref, st_ref, *, rows_in, rows):
    c = pl.program_id(2)
    nc = pl.num_programs(2)

    @pl.when(c == 0)
    def _():
        st_ref[...] = s0_ref[0, 0].T

    lg = lbl_ref[...]
    lge = jnp.exp(lg - jnp.max(lg, axis=0, keepdims=True))
    lb = lge[0:1, :] / jnp.sum(lge, axis=0, keepdims=True)

    hq, hf, v = q_ref[...], f_ref[...], i_ref[...]
    f = lb + (1.0 - lb) * _sigmoid(hf)
    q = hq * _sigmoid(hq)
    k = 1.0 - f
    g = jnp.log(f)
    if rows > rows_in:
        pad = jnp.zeros((rows - rows_in, HEAD_DIM), F32)
        q, k, g, v = (jnp.concatenate([a, pad], axis=0) for a in (q, k, g, v))
    b = _cumsum_rows(g)

    m = HGRN_SUB
    ns = rows // m
    row_m = lax.broadcasted_iota(jnp.int32, (m, HEAD_DIM), 0)

    st = st_ref[...]
    o = _nt((q * jnp.exp(b)).astype(BF16), st.astype(BF16))

    o_diag = []
    for i in range(ns):
        sl = slice(i * m, (i + 1) * m)
        qi, ki, bi, vi = q[sl], k[sl], b[sl], v[sl]
        oi = jnp.zeros((m, HEAD_DIM), F32)
        for s in range(m):
            e = jnp.exp(jnp.where(row_m >= s, bi - bi[s:s + 1], NEG_INF))
            a = jnp.sum(qi * e * ki[s:s + 1], axis=-1, keepdims=True)
            oi = oi + a * vi[s:s + 1]
        o_diag.append(oi)
    o = o + jnp.concatenate(o_diag, axis=0) if ns > 1 else o + o_diag[0]

    for j in range(ns - 1):
        lo = (j + 1) * m
        r = b[lo - 1:lo]
        kt = k[j * m:lo] * jnp.exp(r - b[j * m:lo])
        qt = q[lo:] * jnp.exp(b[lo:] - r)
        a = _nt(qt.astype(BF16), kt.astype(BF16))
        contrib = jnp.dot(a.astype(BF16), v[j * m:lo].astype(BF16), preferred_element_type=F32)
        o = o + jnp.concatenate([jnp.zeros((lo, HEAD_DIM), F32), contrib], axis=0)

    o_ref[...] = o[:rows_in]

    b_last = b[rows - 1:rows]
    kdec = k * jnp.exp(b_last - b)
    vp, kp = v, kdec
    if rows < HEAD_DIM:
        zp = jnp.zeros((HEAD_DIM - rows, HEAD_DIM), F32)
        vp, kp = jnp.concatenate([v, zp], axis=0), jnp.concatenate([kdec, zp], axis=0)
    st_new = st * jnp.exp(b_last) + jnp.dot(vp.T.astype(BF16), kp.astype(BF16),
                                             preferred_element_type=F32)
    st_ref[...] = st_new

    @pl.when(c == nc - 1)
    def _():
        sout_ref[0, 0] = st_new.T


def hgrn2(z, lb_logits, s0, rows_in, rows):
    bsz = s0.shape[0]
    t = z.shape[0] // bsz
    nc = t // rows_in
    kern = functools.partial(_hgrn_kernel, rows_in=rows_in, rows=rows)

    def zspec(off):
        return pl.BlockSpec((rows_in, HEAD_DIM), lambda b, h, c: (b * nc + c, off + h))

    sspec = pl.BlockSpec((1, 1, HEAD_DIM, HEAD_DIM), lambda b, h, c: (b, h, 0, 0))
    return pl.pallas_call(
        kern,
        out_shape=(jax.ShapeDtypeStruct((bsz * t, W_GROUP), F32),
                   jax.ShapeDtypeStruct(s0.shape, s0.dtype)),
        grid=(bsz, N_HGRN_HEADS, nc),
        in_specs=[zspec(ZQ), zspec(ZF), zspec(ZI),
                  pl.BlockSpec((lb_logits.shape[0], HEAD_DIM), lambda b, h, c: (0, h)),
                  sspec],
        out_specs=(pl.BlockSpec((rows_in, HEAD_DIM), lambda b, h, c: (b * nc + c, h)), sspec),
        scratch_shapes=[pltpu.VMEM((HEAD_DIM, HEAD_DIM), F32)],
        compiler_params=_cparams(("arbitrary", "arbitrary", "arbitrary")),
        name="hgrn2",
    )(z, z, z, lb_logits, s0)


def _topk_rows_mask(g, topk):
    nb = g.shape[0]
    jj = lax.broadcasted_iota(jnp.int32, g.shape, 0)
    sel = jnp.zeros(g.shape, F32)
    for _ in range(topk):
        mx = jnp.max(g, axis=0, keepdims=True)
        idx = jnp.min(jnp.where(g == mx, jj, nb), axis=0, keepdims=True)
        hit = jj == idx
        sel = jnp.where(jnp.logical_and(hit, mx > NEG_INF), 1.0, sel)
        g = jnp.where(hit, NEG_INF, g)
    return sel


def _moba_prompt_kernel(slope_ref, q_ref, k_ref, v_ref, o_ref,
                        kb_ref, vt_ref, kmean_ref, sd_ref, sel_ref, *, nb, topk):
    blk = MOBA_BLOCK
    h = pl.program_id(0)
    qb = pl.program_id(1)
    slope = slope_ref[h]

    @pl.when(qb == 0)
    def _():
        def prep(j, carry):
            r0 = pl.multiple_of(j * blk, blk)
            kj = k_ref[pl.ds(r0, blk), :]
            kb_ref[pl.ds(r0, blk), :] = kj.astype(BF16)
            vt_ref[:, pl.ds(r0, blk)] = v_ref[pl.ds(r0, blk), :].T.astype(BF16)
            rowj = lax.broadcasted_iota(jnp.int32, (nb, HEAD_DIM), 0)
            kmean_ref[...] = jnp.where(rowj == j, jnp.mean(kj, axis=0, keepdims=True), kmean_ref[...])
            return carry
        lax.fori_loop(0, nb, prep, 0)
        tt = lax.broadcasted_iota(jnp.int32, (blk, blk), 1)
        rr = lax.broadcasted_iota(jnp.int32, (blk, blk), 0)
        sd_ref[...] = slope * (tt - rr).astype(F32)

    q = q_ref[...] * ATT_SCALE
    q16 = q.astype(BF16)

    gate = _nt(kmean_ref[...], q, precision=lax.Precision.HIGHEST)
    jj = lax.broadcasted_iota(jnp.int32, (nb, blk), 0)
    sel_ref[...] = _topk_rows_mask(jnp.where(jj < qb, gate, NEG_INF), topk)

    r0 = pl.multiple_of(qb * blk, blk)
    tt = lax.broadcasted_iota(jnp.int32, (blk, blk), 1)
    rr = lax.broadcasted_iota(jnp.int32, (blk, blk), 0)
    s = _nt(kb_ref[pl.ds(r0, blk), :], q16) - sd_ref[...]
    s = jnp.where(tt >= rr, s, NEG_INF)
    m0 = jnp.max(s, axis=0, keepdims=True)
    p = jnp.exp(s - m0)
    l0 = jnp.sum(p, axis=0, keepdims=True)
    acc0 = jnp.dot(vt_ref[:, pl.ds(r0, blk)], p.astype(BF16), preferred_element_type=F32)

    def body(j, carry):
        m, l, acc = carry
        c0 = pl.multiple_of(j * blk, blk)
        off = ((qb - j) * blk).astype(F32)
        s = _nt(kb_ref[pl.ds(c0, blk), :], q16) - sd_ref[...] - slope * off
        s = jnp.where(sel_ref[pl.ds(j, 1), :] > 0.0, s, NEG_INF)
        mn = jnp.maximum(m, jnp.max(s, axis=0, keepdims=True))
        alpha = jnp.exp(m - mn)
        p = jnp.exp(s - mn)
        l = alpha * l + jnp.sum(p, axis=0, keepdims=True)
        acc = acc * alpha + jnp.dot(vt_ref[:, pl.ds(c0, blk)], p.astype(BF16),
                                    preferred_element_type=F32)
        return mn, l, acc

    _, l, acc = lax.fori_loop(0, qb, body, (m0, l0, acc0))
    o_ref[...] = (acc / l).T


def moba_prompt(z, k, v, slopes):
    t = k.shape[0]
    nb = t // MOBA_BLOCK
    topk = min(MOBA_TOPK, nb - 1)
    kern = functools.partial(_moba_prompt_kernel, nb=nb, topk=topk)
    grid_spec = pltpu.PrefetchScalarGridSpec(
        num_scalar_prefetch=1,
        grid=(N_ATT_HEADS, nb),
        in_specs=[pl.BlockSpec((MOBA_BLOCK, HEAD_DIM), lambda h, qb, sl: (qb, ZA + h)),
                  pl.BlockSpec((t, HEAD_DIM), lambda h, qb, sl: (0, h)),
                  pl.BlockSpec((t, HEAD_DIM), lambda h, qb, sl: (0, h))],
        out_specs=pl.BlockSpec((MOBA_BLOCK, HEAD_DIM), lambda h, qb, sl: (qb, h)),
        scratch_shapes=[pltpu.VMEM((t, HEAD_DIM), BF16),
                        pltpu.VMEM((HEAD_DIM, t), BF16),
                        pltpu.VMEM((nb, HEAD_DIM), F32),
                        pltpu.VMEM((MOBA_BLOCK, MOBA_BLOCK), F32),
                        pltpu.VMEM((nb, MOBA_BLOCK), F32)],
    )
    return pl.pallas_call(
        kern,
        out_shape=jax.ShapeDtypeStruct((t, N_ATT_HEADS * HEAD_DIM), F32),
        grid_spec=grid_spec,
        compiler_params=_cparams(("arbitrary", "arbitrary")),
        name="moba_prompt",
    )(slopes, z, k, v)


def _lane_topk_mask(g, topk, n_valid):
    lane = lax.broadcasted_iota(jnp.int32, g.shape, 1).astype(F32)
    g = jnp.where(lane < n_valid, g, NEG_INF)
    sel = jnp.zeros(g.shape, F32)
    for _ in range(topk):
        mx = jnp.max(g, axis=1, keepdims=True)
        idx = jnp.min(jnp.where(g == mx, lane, float(g.shape[1])), axis=1, keepdims=True)
        hit = lane == idx
        sel = jnp.where(jnp.logical_and(hit, mx > NEG_INF), 1.0, sel)
        g = jnp.where(hit, NEG_INF, g)
    return sel


def _moba_sample_kernel(pt_ref, slope_ref, q_ref, kn_ref, vn_ref, ck0_ref, ck1_ref, cv0_ref, cv1_ref,
                        o_ref, qf_ref, bias_ref, ks_ref, m_ref, l_ref, acc_ref, *, nb, nt, past_len):
    del pt_ref
    blk = MOBA_BLOCK
    nh = N_ATT_HEADS
    nl = nh * nt
    wide = blk * nh
    j = pl.program_id(1)
    slope = slope_ref[...]

    @pl.when(j == 0)
    def _():
        aq = q_ref[...] * ATT_SCALE
        qf_ref[...] = jnp.concatenate(
            [aq[:, h * HEAD_DIM:(h + 1) * HEAD_DIM] for h in range(nh)], axis=0)
        qrow = lax.broadcasted_iota(jnp.int32, (nl, wide), 0)
        kcol = lax.broadcasted_iota(jnp.int32, (nl, wide), 1)
        same_head = kcol % nh == qrow // nt
        bias_ref[...] = jnp.where(same_head, -slope * (qrow % nt - kcol // nh).astype(F32), NEG_INF)
        m_ref[...] = jnp.zeros(m_ref.shape, F32)
        l_ref[...] = jnp.zeros(l_ref.shape, F32)

    kblk = jnp.concatenate([ck0_ref[0], ck1_ref[0]], axis=0)
    vblk = jnp.concatenate([cv0_ref[0], cv1_ref[0]], axis=0)
    ks_ref[pl.ds(pl.multiple_of(j * nh, nh), nh), :] = jnp.sum(kblk.reshape(blk, nh, HEAD_DIM), axis=0)

    s = _nt(qf_ref[...].astype(BF16), kblk.astype(BF16)) + bias_ref[...]
    mj = jnp.max(s, axis=1, keepdims=True)
    p = jnp.exp(s - mj)
    lj = jnp.sum(p, axis=1, keepdims=True)
    acc_ref[j] = jnp.dot(p.astype(BF16), vblk.astype(BF16), preferred_element_type=F32)
    lane = lax.broadcasted_iota(jnp.int32, (nl, HEAD_DIM), 1)
    off = (past_len - j * blk).astype(F32)
    m_ref[...] = jnp.where(lane == j, mj - slope * off, m_ref[...])
    l_ref[...] = jnp.where(lane == j, lj, l_ref[...])

    @pl.when(j == nb - 1)
    def _():
        qf = qf_ref[...]
        q16 = qf.astype(BF16)
        qrow = lax.broadcasted_iota(jnp.int32, (nl, nb * nh), 0)
        gcol = lax.broadcasted_iota(jnp.int32, (nl, nb * nh), 1)
        g_all = _nt(qf, ks_ref[...] * (1.0 / blk), precision=lax.Precision.HIGHEST)
        g_own = jnp.where(gcol % nh == qrow // nt, g_all, 0.0)
        erow = lax.broadcasted_iota(jnp.int32, (nb * nh, HEAD_DIM), 0)
        ecol = lax.broadcasted_iota(jnp.int32, (nb * nh, HEAD_DIM), 1)
        pick = jnp.where(erow // nh == ecol, 1.0, 0.0)
        gate = jnp.dot(g_own, pick, preferred_element_type=F32, precision=lax.Precision.HIGHEST)
        w = _lane_topk_mask(gate, min(MOBA_TOPK, nb), nb)

        orow = lax.broadcasted_iota(jnp.int32, (nl, nl), 0)
        ocol = lax.broadcasted_iota(jnp.int32, (nl, nl), 1)
        dist = orow % nt - ocol // nh
        ok = jnp.logical_and(ocol % nh == orow // nt, dist >= 0)
        so = jnp.where(ok, _nt(q16, kn_ref[...].astype(BF16)) - slope * dist.astype(F32), NEG_INF)
        mo = jnp.max(so, axis=1, keepdims=True)

        mall = m_ref[...]
        mtot = jnp.maximum(mo, jnp.max(jnp.where(w > 0.0, mall, NEG_INF), axis=1, keepdims=True))
        wj = jnp.where(w > 0.0, jnp.exp(mall - mtot), 0.0)
        po = jnp.exp(so - mtot)
        denom = jnp.sum(wj * l_ref[...], axis=1, keepdims=True) + jnp.sum(po, axis=1, keepdims=True)
        num = jnp.dot(po.astype(BF16), vn_ref[...].astype(BF16), preferred_element_type=F32)
        for jb in range(nb):
            num = num + wj[:, jb:jb + 1] * acc_ref[jb]
        out = num / denom
        for h in range(nh):
            o_ref[:, h * HEAD_DIM:(h + 1) * HEAD_DIM] = out[h * nt:(h + 1) * nt, :]


def moba_sample(z, k_new, v_new, cache_k, cache_v, page_table, slopes, past_len):
    bsz, n_pages = page_table.shape
    nh = N_ATT_HEADS
    nt = k_new.shape[0] // (bsz * nh)
    nb = past_len // MOBA_BLOCK
    assert n_pages == nb * PAGES_PER_BLOCK and PAGES_PER_BLOCK == 2
    assert nh * nt == HEAD_DIM and nb <= HEAD_DIM
    wd = nh * HEAD_DIM
    slope_col = jnp.repeat(slopes, nt).reshape(nh * nt, 1)
    kern = functools.partial(_moba_sample_kernel, nb=nb, nt=nt, past_len=past_len)

    def page_spec(which):
        return pl.BlockSpec((1, PAGE_SIZE * nh, HEAD_DIM),
                            lambda b, j, pt: (pt[b * n_pages + 2 * j + which], 0, 0))

    new_spec = pl.BlockSpec((nt * nh, HEAD_DIM), lambda b, j, pt: (b, 0))
    grid_spec = pltpu.PrefetchScalarGridSpec(
        num_scalar_prefetch=1,
        grid=(bsz, nb),
        in_specs=[pl.BlockSpec((nh * nt, 1), lambda b, j, pt: (0, 0)),
                  pl.BlockSpec((nt, wd), lambda b, j, pt: (b, ZA * HEAD_DIM // wd)),
                  new_spec, new_spec,
                  page_spec(0), page_spec(1), page_spec(0), page_spec(1)],
        out_specs=pl.BlockSpec((nt, wd), lambda b, j, pt: (b, 0)),
        scratch_shapes=[pltpu.VMEM((nh * nt, HEAD_DIM), F32),
                        pltpu.VMEM((nh * nt, MOBA_BLOCK * nh), F32),
                        pltpu.VMEM((nb * nh, HEAD_DIM), F32),
                        pltpu.VMEM((nh * nt, HEAD_DIM), F32),
                        pltpu.VMEM((nh * nt, HEAD_DIM), F32),
                        pltpu.VMEM((nb, nh * nt, HEAD_DIM), F32)],
    )
    return pl.pallas_call(
        kern,
        out_shape=jax.ShapeDtypeStruct((bsz * nt, wd), F32),
        grid_spec=grid_spec,
        compiler_params=_cparams(("arbitrary", "arbitrary")),
        name="moba_sample",
    )(page_table.reshape(-1), slope_col, z, k_new, v_new, cache_k, cache_k, cache_v, cache_v)


def _mix_out_kernel(oh_ref, g_ref, oa_ref, x_ref, w_ref, hn_ref, an_ref, pm_ref, pf_ref,
                    x1_ref, h2_ref, cat_ref, mix_ref, *, tn):
    j = pl.program_id(1)
    nj = pl.num_programs(1)

    @pl.when(j == 0)
    def _():
        gate = g_ref[...]
        hn = hn_ref[...]
        for h in range(N_HGRN_HEADS):
            sl = slice(h * HEAD_DIM, (h + 1) * HEAD_DIM)
            o = oh_ref[:, sl]
            y = o * lax.rsqrt(jnp.mean(o * o, axis=-1, keepdims=True) + EPS) * hn
            gh = gate[:, sl]
            cat_ref[:, sl] = (y * (gh * _sigmoid(gh))).astype(BF16)
        oa = oa_ref[...]
        ya = oa * lax.rsqrt(jnp.mean(oa * oa, axis=-1, keepdims=True) + EPS) * an_ref[...]
        cat_ref[:, W_GROUP:] = ya.astype(BF16)

    c0 = pl.multiple_of(j * tn, tn)
    mix_ref[:, pl.ds(c0, tn)] = jnp.dot(cat_ref[...], w_ref[...], preferred_element_type=F32)

    @pl.when(j == nj - 1)
    def _():
        mix = mix_ref[...]
        x1 = x_ref[...] + mix * lax.rsqrt(jnp.mean(mix * mix, axis=-1, keepdims=True) + EPS) * pm_ref[...]
        x1_ref[...] = x1
        h2 = x1 * lax.rsqrt(jnp.mean(x1 * x1, axis=-1, keepdims=True) + EPS) * pf_ref[...]
        h2_ref[...] = h2.astype(BF16)


def mix_out(o_hgrn, z, o_att, x, w_out, hgrn_out_norm, attn_out_norm, norm_post_mix, norm_pre_ffn,
            tm, tn):
    m, d = x.shape
    kern = functools.partial(_mix_out_kernel, tn=tn)
    gsec = ZG * HEAD_DIM // W_GROUP
    vec = lambda n: pl.BlockSpec((1, n), lambda i, j: (0, 0))
    return pl.pallas_call(
        kern,
        out_shape=(jax.ShapeDtypeStruct((m, d), F32), jax.ShapeDtypeStruct((m, d), BF16)),
        grid=(m // tm, d // tn),
        in_specs=[pl.BlockSpec((tm, W_GROUP), lambda i, j: (i, 0)),
                  pl.BlockSpec((tm, W_GROUP), lambda i, j: (i, gsec)),
                  pl.BlockSpec((tm, W_GROUP), lambda i, j: (i, 0)),
                  pl.BlockSpec((tm, d), lambda i, j: (i, 0)),
                  pl.BlockSpec((2 * W_GROUP, tn), lambda i, j: (0, j)),
                  vec(HEAD_DIM), vec(W_GROUP), vec(d), vec(d)],
        out_specs=(pl.BlockSpec((tm, d), lambda i, j: (i, 0)),
                   pl.BlockSpec((tm, d), lambda i, j: (i, 0))),
        scratch_shapes=[pltpu.VMEM((tm, 2 * W_GROUP), BF16), pltpu.VMEM((tm, d), F32)],
        compiler_params=_cparams(("arbitrary", "arbitrary")),
        name="mix_out",
    )(o_hgrn, z, o_att, x, w_out, hgrn_out_norm.reshape(1, -1), attn_out_norm.reshape(1, -1),
      norm_post_mix.reshape(1, -1), norm_pre_ffn.reshape(1, -1))


def _ffn_kernel(h_ref, wg_ref, wu_ref, wd_ref, x1_ref, pn_ref, y_ref, acc_ref):
    f = pl.program_id(1)
    nf = pl.num_programs(1)
    h = h_ref[...]
    g = jnp.dot(h, wg_ref[...], preferred_element_type=F32)
    u = jnp.dot(h, wu_ref[...], preferred_element_type=F32)
    a = (g * _sigmoid(g) * u).astype(BF16)
    part = jnp.dot(a, wd_ref[...], preferred_element_type=F32)

    @pl.when(f == 0)
    def _():
        acc_ref[...] = part

    @pl.when(f > 0)
    def _():
        acc_ref[...] += part

    @pl.when(f == nf - 1)
    def _():
        ff = acc_ref[...]
        y_ref[...] = x1_ref[...] + ff * lax.rsqrt(jnp.mean(ff * ff, axis=-1, keepdims=True) + EPS) * pn_ref[...]


def ffn(h2, x1, w_gate, w_up, w_down, norm_post_ffn, tm, tf):
    m, d = h2.shape
    dff = w_gate.shape[1]
    return pl.pallas_call(
        _ffn_kernel,
        out_shape=jax.ShapeDtypeStruct((m, d), F32),
        grid=(m // tm, dff // tf),
        in_specs=[pl.BlockSpec((tm, d), lambda i, f: (i, 0)),
                  pl.BlockSpec((d, tf), lambda i, f: (0, f)),
                  pl.BlockSpec((d, tf), lambda i, f: (0, f)),
                  pl.BlockSpec((tf, d), lambda i, f: (f, 0)),
                  pl.BlockSpec((tm, d), lambda i, f: (i, 0)),
                  pl.BlockSpec((1, d), lambda i, f: (0, 0))],
        out_specs=pl.BlockSpec((tm, d), lambda i, f: (i, 0)),
        scratch_shapes=[pltpu.VMEM((tm, d), F32)],
        compiler_params=_cparams(("arbitrary", "arbitrary")),
        name="ffn",
    )(h2, w_gate, w_up, w_down, x1, norm_post_ffn.reshape(1, -1))


def _alibi_slopes(n):
    return 2.0 ** (-8.0 * jnp.arange(1, n + 1, dtype=F32) / n)


def _token_tile(m, cap):
    return cap if m % cap == 0 else m


def _project(x2d, norm_pre_mix, w_in16):
    m = x2d.shape[0]
    h = rmsnorm_bf16(x2d, norm_pre_mix, _token_tile(m, 256))
    tm = _token_tile(m, 1024)
    z = matmul_cols(h, w_in16, 0, 5 * W_GROUP, tm, 1024)
    k = matmul_cols(h, w_in16, 5 * W_GROUP, W_GROUP, tm, 1024)
    v = matmul_cols(h, w_in16, 6 * W_GROUP, W_GROUP, tm, 1024)
    return z, k, v


def _finish(x2d, o_hgrn, z, o_att, w16, norms):
    hgrn_out_norm, attn_out_norm, norm_post_mix, norm_pre_ffn, norm_post_ffn = norms
    w_out16, w_gate16, w_up16, w_down16 = w16
    m = x2d.shape[0]
    tm = _token_tile(m, 256)
    x1, h2 = mix_out(o_hgrn, z, o_att, x2d, w_out16, hgrn_out_norm, attn_out_norm, norm_post_mix,
                     norm_pre_ffn, tm, 512)
    return ffn(h2, x1, w_gate16, w_up16, w_down16, norm_post_ffn, tm, 256)


def kernel(x_prompt, x_sample, cache_k, cache_v, page_table, state_hgrn, norm_pre_mix, w_in,
           hgrn_lb_logits, hgrn_out_norm, attn_out_norm, w_out, norm_post_mix, norm_pre_ffn,
           w_gate, w_up, w_down, norm_post_ffn):
    bp, tp, d = x_prompt.shape
    bs, ts, _ = x_sample.shape
    assert bp == 1
    n_pool, page, h_att, hd = cache_k.shape
    past_len = page_table.shape[1] * page
    slopes = _alibi_slopes(N_ATT_HEADS)
    w_in16, w_out16, w_gate16, w_up16, w_down16 = (
        w.astype(BF16) for w in (w_in, w_out, w_gate, w_up, w_down))
    w16 = (w_out16, w_gate16, w_up16, w_down16)
    norms = (hgrn_out_norm, attn_out_norm, norm_post_mix, norm_pre_ffn, norm_post_ffn)

    xp = x_prompt.reshape(tp, d)
    zp, kp, vp = _project(xp, norm_pre_mix, w_in16)
    s0 = jnp.zeros((bp,) + state_hgrn.shape[1:], state_hgrn.dtype)
    oh_p, state_prompt = hgrn2(zp, hgrn_lb_logits, s0, HEAD_DIM, HEAD_DIM)
    oa_p = moba_prompt(zp, kp, vp, slopes)
    y_prompt = _finish(xp, oh_p, zp, oa_p, w16, norms)

    xs = x_sample.reshape(bs * ts, d)
    zs, ks, vs = _project(xs, norm_pre_mix, w_in16)
    oh_s, state_sample = hgrn2(zs, hgrn_lb_logits, state_hgrn, ts, HGRN_SUB)
    oa_s = moba_sample(zs, ks.reshape(bs * ts * h_att, hd), vs.reshape(bs * ts * h_att, hd),
                       cache_k.reshape(n_pool, page * h_att, hd),
                       cache_v.reshape(n_pool, page * h_att, hd), page_table, slopes, past_len)
    y_sample = _finish(xs, oh_s, zs, oa_s, w16, norms)

    return (y_prompt.reshape(bp, tp, d), y_sample.reshape(bs, ts, d),
            kp.reshape(bp, tp, h_att, hd), vp.reshape(bp, tp, h_att, hd),
            ks.reshape(bs, ts, h_att, hd), vs.reshape(bs, ts, h_att, hd),
            state_prompt, state_sample)
```

```python
import functools

import jax
import jax.numpy as jnp
from jax import lax
from jax.experimental import pallas as pl
from jax.experimental.pallas import tpu as pltpu

F32 = jnp.float32
BF16 = jnp.bfloat16

EPS = 1e-6
HEAD_DIM = 128
N_HGRN_HEADS = 16
N_ATT_HEADS = 16
W_GROUP = N_HGRN_HEADS * HEAD_DIM
MOBA_BLOCK = 256
MOBA_TOPK = 3
PAGE_SIZE = 128
PAGES_PER_BLOCK = MOBA_BLOCK // PAGE_SIZE
ATT_SCALE = HEAD_DIM ** -0.5
HGRN_SUB = 16
NEG_INF = float("-inf")
LOG2E = 1.4426950408889634
MOBA_GROUP = 8
VMEM_LIMIT = 56 * 1024 * 1024

ZQ, ZF, ZI, ZG, ZA = 0, 16, 32, 48, 64


def _cparams(sem):
    return pltpu.CompilerParams(dimension_semantics=sem, vmem_limit_bytes=VMEM_LIMIT)


def _nt(a, b, precision=None):
    return lax.dot_general(a, b, (((1,), (1,)), ((), ())), preferred_element_type=F32,
                           precision=precision)


def _sigmoid(x):
    return 1.0 / (1.0 + jnp.exp(-x))


def _norm_kernel(x_ref, g_ref, o_ref):
    x = x_ref[...]
    y = x * lax.rsqrt(jnp.mean(x * x, axis=-1, keepdims=True) + EPS)
    o_ref[...] = (y * g_ref[...]).astype(BF16)


def rmsnorm_bf16(x, g, tm):
    m, d = x.shape
    return pl.pallas_call(
        _norm_kernel,
        out_shape=jax.ShapeDtypeStruct((m, d), BF16),
        grid=(m // tm,),
        in_specs=[pl.BlockSpec((tm, d), lambda i: (i, 0)), pl.BlockSpec((1, d), lambda i: (0, 0))],
        out_specs=pl.BlockSpec((tm, d), lambda i: (i, 0)),
        compiler_params=_cparams(("arbitrary",)),
        name="rmsnorm_bf16",
    )(x, g.reshape(1, d))


def _mm_kernel(a_ref, w_ref, o_ref):
    o_ref[...] = jnp.dot(a_ref[...], w_ref[...], preferred_element_type=F32)


def matmul_cols(a, w, col0, n, tm, tn):
    m, k = a.shape
    cb = col0 // tn
    return pl.pallas_call(
        _mm_kernel,
        out_shape=jax.ShapeDtypeStruct((m, n), F32),
        grid=(m // tm, n // tn),
        in_specs=[pl.BlockSpec((tm, k), lambda i, j: (i, 0)),
                  pl.BlockSpec((k, tn), lambda i, j: (0, cb + j))],
        out_specs=pl.BlockSpec((tm, tn), lambda i, j: (i, j)),
        compiler_params=_cparams(("arbitrary", "arbitrary")),
        name="matmul_cols",
    )(a, w)


def _cumsum_rows(g):
    c = g.shape[0]
    row = lax.broadcasted_iota(jnp.int32, g.shape, 0)
    b = g
    sh = 1
    while sh < c:
        b = b + jnp.where(row >= sh, pltpu.roll(b, sh, 0), 0.0)
        sh *= 2
    return b


def _hgrn_kernel(q_ref, f_ref, i_ref, lbl_ref, s0_ref, o_ref, sout_ref, st_ref, *, rows_in, rows):
    c = pl.program_id(2)
    nc = pl.num_programs(2)

    @pl.when(c == 0)
    def _():
        st_ref[...] = s0_ref[0, 0].T

    lg = lbl_ref[...]
    lge = jnp.exp(lg - jnp.max(lg, axis=0, keepdims=True))
    lb = lge[0:1, :] / jnp.sum(lge, axis=0, keepdims=True)

    hq, hf, v = q_ref[...], f_ref[...], i_ref[...]
    f = lb + (1.0 - lb) * _sigmoid(hf)
    q = hq * _sigmoid(hq)
    k = 1.0 - f
    g = jnp.log(f)
    if rows > rows_in:
        pad = jnp.zeros((rows - rows_in, HEAD_DIM), F32)
        q, k, g, v = (jnp.concatenate([a, pad], axis=0) for a in (q, k, g, v))
    b = _cumsum_rows(g)

    m = HGRN_SUB
    ns = rows // m

    st = st_ref[...]
    o = _nt((q * jnp.exp(b)).astype(BF16), st.astype(BF16))

    def exact_diag(width):
        row_w = lax.broadcasted_iota(jnp.int32, (width, HEAD_DIM), 0)
        outs = []
        for i in range(rows // width):
            sl = slice(i * width, (i + 1) * width)
            qi, ki, bi, vi = q[sl], k[sl], b[sl], v[sl]
            oi = jnp.zeros((width, HEAD_DIM), F32)
            for s in range(width):
                e = jnp.exp(jnp.where(row_w >= s, bi - bi[s:s + 1], NEG_INF))
                a = jnp.sum(qi * e * ki[s:s + 1], axis=-1, keepdims=True)
                oi = oi + a * vi[s:s + 1]
            outs.append(oi)
        return outs

    half = m // 2
    if ns >= 2:
        o_diag = exact_diag(half)
        q3, k3, b3, v3 = (a.reshape(ns, m, HEAD_DIM) for a in (q, k, b, v))
        r = b3[:, half - 1:half, :]
        qt = (q3[:, half:, :] * jnp.exp(b3[:, half:, :] - r)).reshape(ns * half, HEAD_DIM)
        kt = (k3[:, :half, :] * jnp.exp(r - b3[:, :half, :])).reshape(ns * half, HEAD_DIM)
        a = _nt(qt.astype(BF16), kt.astype(BF16))
        arow = lax.broadcasted_iota(jnp.int32, a.shape, 0) // half
        acol = lax.broadcasted_iota(jnp.int32, a.shape, 1) // half
        a = jnp.where(arow == acol, a, 0.0)
        low = jnp.dot(a.astype(BF16), v3[:, :half, :].reshape(ns * half, HEAD_DIM).astype(BF16),
                      preferred_element_type=F32)
        for i in range(ns):
            o_diag[2 * i + 1] = o_diag[2 * i + 1] + low[i * half:(i + 1) * half]
    else:
        o_diag = exact_diag(m)
    o = o + jnp.concatenate(o_diag, axis=0) if len(o_diag) > 1 else o + o_diag[0]

    for j in range(ns - 1):
        lo = (j + 1) * m
        r = b[lo - 1:lo]
        kt = k[j * m:lo] * jnp.exp(r - b[j * m:lo])
        qt = q[lo:] * jnp.exp(b[lo:] - r)
        a = _nt(qt.astype(BF16), kt.astype(BF16))
        contrib = jnp.dot(a.astype(BF16), v[j * m:lo].astype(BF16), preferred_element_type=F32)
        o = o + jnp.concatenate([jnp.zeros((lo, HEAD_DIM), F32), contrib], axis=0)

    o_ref[...] = o[:rows_in]

    b_last = b[rows - 1:rows]
    kdec = k * jnp.exp(b_last - b)
    vp, kp = v, kdec
    if rows < HEAD_DIM:
        zp = jnp.zeros((HEAD_DIM - rows, HEAD_DIM), F32)
        vp, kp = jnp.concatenate([v, zp], axis=0), jnp.concatenate([kdec, zp], axis=0)
    st_new = st * jnp.exp(b_last) + jnp.dot(vp.T.astype(BF16), kp.astype(BF16),
                                             preferred_element_type=F32)
    st_ref[...] = st_new

    @pl.when(c == nc - 1)
    def _():
        sout_ref[0, 0] = st_new.T


def hgrn2(z, lb_logits, s0, rows_in, rows):
    bsz = s0.shape[0]
    t = z.shape[0] // bsz
    nc = t // rows_in
    kern = functools.partial(_hgrn_kernel, rows_in=rows_in, rows=rows)

    def zspec(off):
        return pl.BlockSpec((rows_in, HEAD_DIM), lambda b, h, c: (b * nc + c, off + h))

    sspec = pl.BlockSpec((1, 1, HEAD_DIM, HEAD_DIM), lambda b, h, c: (b, h, 0, 0))
    return pl.pallas_call(
        kern,
        out_shape=(jax.ShapeDtypeStruct((bsz * t, W_GROUP), F32),
                   jax.ShapeDtypeStruct(s0.shape, s0.dtype)),
        grid=(bsz, N_HGRN_HEADS, nc),
        in_specs=[zspec(ZQ), zspec(ZF), zspec(ZI),
                  pl.BlockSpec((lb_logits.shape[0], HEAD_DIM), lambda b, h, c: (0, h)),
                  sspec],
        out_specs=(pl.BlockSpec((rows_in, HEAD_DIM), lambda b, h, c: (b * nc + c, h)), sspec),
        scratch_shapes=[pltpu.VMEM((HEAD_DIM, HEAD_DIM), F32)],
        compiler_params=_cparams(("arbitrary", "arbitrary", "arbitrary")),
        name="hgrn2",
    )(z, z, z, lb_logits, s0)


def _topk_rows_mask(g, topk):
    nb = g.shape[0]
    jj = lax.broadcasted_iota(jnp.int32, g.shape, 0)
    sel = jnp.zeros(g.shape, F32)
    for _ in range(topk):
        mx = jnp.max(g, axis=0, keepdims=True)
        idx = jnp.min(jnp.where(g == mx, jj, nb), axis=0, keepdims=True)
        hit = jj == idx
        sel = jnp.where(jnp.logical_and(hit, mx > NEG_INF), 1.0, sel)
        g = jnp.where(hit, NEG_INF, g)
    return sel


def _split3_bf16(x):
    a = x.astype(BF16)
    r = x - a.astype(F32)
    b = r.astype(BF16)
    return a, b, (r - b.astype(F32)).astype(BF16)


def _moba_prompt_kernel(slope_ref, q_ref, k_ref, v_ref, o_ref,
                        ka_ref, vt_ref, kmean_ref, qa_ref, sel_ref, *, nb, topk, group):
    blk = MOBA_BLOCK
    h = pl.program_id(0)
    qb = pl.program_id(1)
    slope2 = slope_ref[h] * LOG2E
    lane = lax.broadcasted_iota(jnp.int32, (blk, HEAD_DIM), 1)

    @pl.when(qb == 0)
    def _():
        rloc = lax.broadcasted_iota(jnp.int32, (blk, HEAD_DIM), 0).astype(F32)
        b1, b2, b3 = _split3_bf16(slope2 * rloc)
        aug = jnp.where(lane == 0, b1.astype(F32),
                        jnp.where(lane == 1, b2.astype(F32),
                                  jnp.where(lane == 2, b3.astype(F32), 0.0))).astype(BF16)

        def prep(j, carry):
            r0 = pl.multiple_of(j * blk, blk)
            kj = k_ref[pl.ds(r0, blk), :]
            ka_ref[pl.ds(r0, blk), :] = jnp.concatenate([kj.astype(BF16), aug], axis=1)
            vt_ref[:, pl.ds(r0, blk)] = v_ref[pl.ds(r0, blk), :].T.astype(BF16)
            rowj = lax.broadcasted_iota(jnp.int32, (nb, HEAD_DIM), 0)
            kmean_ref[...] = jnp.where(rowj == j, jnp.mean(kj, axis=0, keepdims=True), kmean_ref[...])
            return carry
        lax.fori_loop(0, nb, prep, 0)

    q = q_ref[...] * ATT_SCALE
    ones = jnp.where(lane < 3, 1.0, 0.0).astype(BF16)
    qa_ref[...] = jnp.concatenate([(q * LOG2E).astype(BF16), ones], axis=1)

    gate = _nt(kmean_ref[...], q, precision=lax.Precision.HIGHEST)
    jj = lax.broadcasted_iota(jnp.int32, (nb, blk), 0)
    sel_ref[...] = _topk_rows_mask(jnp.where(jj < qb, gate, NEG_INF), topk)

    r0 = pl.multiple_of(qb * blk, blk)
    tt = lax.broadcasted_iota(jnp.int32, (blk, blk), 1)
    rr = lax.broadcasted_iota(jnp.int32, (blk, blk), 0)
    s = jnp.where(tt >= rr, _nt(ka_ref[pl.ds(r0, blk), :], qa_ref[...]), NEG_INF)
    cq = slope2 * (qb * blk).astype(F32)
    m0 = jnp.max(s, axis=0, keepdims=True) + cq
    p = jnp.exp2(s - (m0 - cq))
    l0 = jnp.sum(p, axis=0, keepdims=True)
    acc0 = jnp.dot(vt_ref[:, pl.ds(r0, blk)], p.astype(BF16), preferred_element_type=F32)

    def body(g, carry):
        m, l, acc = carry
        j0 = g * group
        c0 = pl.multiple_of(j0 * blk, group * blk)
        selg = sel_ref[pl.ds(pl.multiple_of(j0, group), group), :]
        s = _nt(ka_ref[pl.ds(c0, group * blk), :], qa_ref[...])
        ons, cjs = [], []
        mn = m
        for b in range(group):
            cj = slope2 * ((j0 + b) * blk).astype(F32)
            on = selg[b:b + 1, :] > 0.0
            cm = jnp.max(s[b * blk:(b + 1) * blk], axis=0, keepdims=True) + cj
            mn = jnp.maximum(mn, jnp.where(on, cm, NEG_INF))
            ons.append(on)
            cjs.append(cj)
        ps = [jnp.exp2(s[b * blk:(b + 1) * blk] - jnp.where(ons[b], mn - cjs[b], float("inf")))
              for b in range(group)]
        alpha = jnp.exp2(m - mn)
        lsum = ps[0].sum(axis=0, keepdims=True)
        for b in range(1, group):
            lsum = lsum + ps[b].sum(axis=0, keepdims=True)
        p = jnp.concatenate([x.astype(BF16) for x in ps], axis=0)
        pv = jnp.dot(vt_ref[:, pl.ds(c0, group * blk)], p, preferred_element_type=F32)
        return mn, l * alpha + lsum, acc * alpha + pv

    _, l, acc = lax.fori_loop(0, (qb + group - 1) // group, body, (m0, l0, acc0))
    o_ref[...] = (acc / l).T


def moba_prompt(z, k, v, slopes):
    t = k.shape[0]
    nb = t // MOBA_BLOCK
    topk = min(MOBA_TOPK, nb - 1)
    group = next(g for g in (MOBA_GROUP, 4, 2, 1) if nb % g == 0)
    kern = functools.partial(_moba_prompt_kernel, nb=nb, topk=topk, group=group)
    grid_spec = pltpu.PrefetchScalarGridSpec(
        num_scalar_prefetch=1,
        grid=(N_ATT_HEADS, nb),
        in_specs=[pl.BlockSpec((MOBA_BLOCK, HEAD_DIM), lambda h, qb, sl: (qb, ZA + h)),
                  pl.BlockSpec((t, HEAD_DIM), lambda h, qb, sl: (0, h)),
                  pl.BlockSpec((t, HEAD_DIM), lambda h, qb, sl: (0, h))],
        out_specs=pl.BlockSpec((MOBA_BLOCK, HEAD_DIM), lambda h, qb, sl: (qb, h)),
        scratch_shapes=[pltpu.VMEM((t, 2 * HEAD_DIM), BF16),
                        pltpu.VMEM((HEAD_DIM, t), BF16),
                        pltpu.VMEM((nb, HEAD_DIM), F32),
                        pltpu.VMEM((MOBA_BLOCK, 2 * HEAD_DIM), BF16),
                        pltpu.VMEM((nb, MOBA_BLOCK), F32)],
    )
    return pl.pallas_call(
        kern,
        out_shape=jax.ShapeDtypeStruct((t, N_ATT_HEADS * HEAD_DIM), F32),
        grid_spec=grid_spec,
        compiler_params=_cparams(("arbitrary", "arbitrary")),
        name="moba_prompt",
    )(slopes, z, k, v)


def _lane_topk_mask(g, topk, n_valid):
    lane = lax.broadcasted_iota(jnp.int32, g.shape, 1).astype(F32)
    g = jnp.where(lane < n_valid, g, NEG_INF)
    sel = jnp.zeros(g.shape, F32)
    for _ in range(topk):
        mx = jnp.max(g, axis=1, keepdims=True)
        idx = jnp.min(jnp.where(g == mx, lane, float(g.shape[1])), axis=1, keepdims=True)
        hit = lane == idx
        sel = jnp.where(jnp.logical_and(hit, mx > NEG_INF), 1.0, sel)
        g = jnp.where(hit, NEG_INF, g)
    return sel


def _moba_sample_kernel(pt_ref, slope_ref, q_ref, kn_ref, vn_ref, ck0_ref, ck1_ref, cv0_ref, cv1_ref,
                        o_ref, qf_ref, bias_ref, ks_ref, m_ref, l_ref, acc_ref, *, nb, nt, past_len):
    del pt_ref
    blk = MOBA_BLOCK
    nh = N_ATT_HEADS
    nl = nh * nt
    wide = blk * nh
    j = pl.program_id(1)
    slope = slope_ref[...]

    @pl.when(j == 0)
    def _():
        aq = q_ref[...] * ATT_SCALE
        qf_ref[...] = jnp.concatenate(
            [aq[:, h * HEAD_DIM:(h + 1) * HEAD_DIM] for h in range(nh)], axis=0)
        qrow = lax.broadcasted_iota(jnp.int32, (nl, wide), 0)
        kcol = lax.broadcasted_iota(jnp.int32, (nl, wide), 1)
        same_head = kcol % nh == qrow // nt
        bias_ref[...] = jnp.where(same_head, -slope * (qrow % nt - kcol // nh).astype(F32), NEG_INF)
        m_ref[...] = jnp.zeros(m_ref.shape, F32)
        l_ref[...] = jnp.zeros(l_ref.shape, F32)

    kblk = jnp.concatenate([ck0_ref[0], ck1_ref[0]], axis=0)
    vblk = jnp.concatenate([cv0_ref[0], cv1_ref[0]], axis=0)
    ks_ref[pl.ds(pl.multiple_of(j * nh, nh), nh), :] = jnp.sum(kblk.reshape(blk, nh, HEAD_DIM), axis=0)

    s = _nt(qf_ref[...].astype(BF16), kblk.astype(BF16)) + bias_ref[...]
    mj = jnp.max(s, axis=1, keepdims=True)
    p = jnp.exp(s - mj)
    lj = jnp.sum(p, axis=1, keepdims=True)
    acc_ref[j] = jnp.dot(p.astype(BF16), vblk.astype(BF16), preferred_element_type=F32)
    lane = lax.broadcasted_iota(jnp.int32, (nl, HEAD_DIM), 1)
    off = (past_len - j * blk).astype(F32)
    m_ref[...] = jnp.where(lane == j, mj - slope * off, m_ref[...])
    l_ref[...] = jnp.where(lane == j, lj, l_ref[...])

    @pl.when(j == nb - 1)
    def _():
        qf = qf_ref[...]
        q16 = qf.astype(BF16)
        qrow = lax.broadcasted_iota(jnp.int32, (nl, nb * nh), 0)
        gcol = lax.broadcasted_iota(jnp.int32, (nl, nb * nh), 1)
        g_all = _nt(qf, ks_ref[...] * (1.0 / blk), precision=lax.Precision.HIGHEST)
        g_own = jnp.where(gcol % nh == qrow // nt, g_all, 0.0)
        erow = lax.broadcasted_iota(jnp.int32, (nb * nh, HEAD_DIM), 0)
        ecol = lax.broadcasted_iota(jnp.int32, (nb * nh, HEAD_DIM), 1)
        pick = jnp.where(erow // nh == ecol, 1.0, 0.0)
        gate = jnp.dot(g_own, pick, preferred_element_type=F32, precision=lax.Precision.HIGHEST)
        w = _lane_topk_mask(gate, min(MOBA_TOPK, nb), nb)

        orow = lax.broadcasted_iota(jnp.int32, (nl, nl), 0)
        ocol = lax.broadcasted_iota(jnp.int32, (nl, nl), 1)
        dist = orow % nt - ocol // nh
        ok = jnp.logical_and(ocol % nh == orow // nt, dist >= 0)
        so = jnp.where(ok, _nt(q16, kn_ref[...].astype(BF16)) - slope * dist.astype(F32), NEG_INF)
        mo = jnp.max(so, axis=1, keepdims=True)

        mall = m_ref[...]
        mtot = jnp.maximum(mo, jnp.max(jnp.where(w > 0.0, mall, NEG_INF), axis=1, keepdims=True))
        wj = jnp.where(w > 0.0, jnp.exp(mall - mtot), 0.0)
        po = jnp.exp(so - mtot)
        denom = jnp.sum(wj * l_ref[...], axis=1, keepdims=True) + jnp.sum(po, axis=1, keepdims=True)
        num = jnp.dot(po.astype(BF16), vn_ref[...].astype(BF16), preferred_element_type=F32)
        for jb in range(nb):
            num = num + wj[:, jb:jb + 1] * acc_ref[jb]
        out = num / denom
        for h in range(nh):
            o_ref[:, h * HEAD_DIM:(h + 1) * HEAD_DIM] = out[h * nt:(h + 1) * nt, :]


def moba_sample(z, k_new, v_new, cache_k, cache_v, page_table, slopes, past_len):
    bsz, n_pages = page_table.shape
    nh = N_ATT_HEADS
    nt = k_new.shape[0] // (bsz * nh)
    nb = past_len // MOBA_BLOCK
    assert n_pages == nb * PAGES_PER_BLOCK and PAGES_PER_BLOCK == 2
    assert nh * nt == HEAD_DIM and nb <= HEAD_DIM
    wd = nh * HEAD_DIM
    slope_col = jnp.repeat(slopes, nt).reshape(nh * nt, 1)
    kern = functools.partial(_moba_sample_kernel, nb=nb, nt=nt, past_len=past_len)

    def page_spec(which):
        return pl.BlockSpec((1, PAGE_SIZE * nh, HEAD_DIM),
                            lambda b, j, pt: (pt[b * n_pages + 2 * j + which], 0, 0))

    new_spec = pl.BlockSpec((nt * nh, HEAD_DIM), lambda b, j, pt: (b, 0))
    grid_spec = pltpu.PrefetchScalarGridSpec(
        num_scalar_prefetch=1,
        grid=(bsz, nb),
        in_specs=[pl.BlockSpec((nh * nt, 1), lambda b, j, pt: (0, 0)),
                  pl.BlockSpec((nt, wd), lambda b, j, pt: (b, ZA * HEAD_DIM // wd)),
                  new_spec, new_spec,
                  page_spec(0), page_spec(1), page_spec(0), page_spec(1)],
        out_specs=pl.BlockSpec((nt, wd), lambda b, j, pt: (b, 0)),
        scratch_shapes=[pltpu.VMEM((nh * nt, HEAD_DIM), F32),
                        pltpu.VMEM((nh * nt, MOBA_BLOCK * nh), F32),
                        pltpu.VMEM((nb * nh, HEAD_DIM), F32),
                        pltpu.VMEM((nh * nt, HEAD_DIM), F32),
                        pltpu.VMEM((nh * nt, HEAD_DIM), F32),
                        pltpu.VMEM((nb, nh * nt, HEAD_DIM), F32)],
    )
    return pl.pallas_call(
        kern,
        out_shape=jax.ShapeDtypeStruct((bsz * nt, wd), F32),
        grid_spec=grid_spec,
        compiler_params=_cparams(("arbitrary", "arbitrary")),
        name="moba_sample",
    )(page_table.reshape(-1), slope_col, z, k_new, v_new, cache_k, cache_k, cache_v, cache_v)


def _mix_out_kernel(oh_ref, g_ref, oa_ref, x_ref, w_ref, hn_ref, an_ref, pm_ref, pf_ref,
                    x1_ref, h2_ref, cat_ref, mix_ref, *, tn):
    j = pl.program_id(1)
    nj = pl.num_programs(1)

    @pl.when(j == 0)
    def _():
        gate = g_ref[...]
        hn = hn_ref[...]
        for h in range(N_HGRN_HEADS):
            sl = slice(h * HEAD_DIM, (h + 1) * HEAD_DIM)
            o = oh_ref[:, sl]
            y = o * lax.rsqrt(jnp.mean(o * o, axis=-1, keepdims=True) + EPS) * hn
            gh = gate[:, sl]
            cat_ref[:, sl] = (y * (gh * _sigmoid(gh))).astype(BF16)
        oa = oa_ref[...]
        ya = oa * lax.rsqrt(jnp.mean(oa * oa, axis=-1, keepdims=True) + EPS) * an_ref[...]
        cat_ref[:, W_GROUP:] = ya.astype(BF16)

    c0 = pl.multiple_of(j * tn, tn)
    mix_ref[:, pl.ds(c0, tn)] = jnp.dot(cat_ref[...], w_ref[...], preferred_element_type=F32)

    @pl.when(j == nj - 1)
    def _():
        mix = mix_ref[...]
        x1 = x_ref[...] + mix * lax.rsqrt(jnp.mean(mix * mix, axis=-1, keepdims=True) + EPS) * pm_ref[...]
        x1_ref[...] = x1
        h2 = x1 * lax.rsqrt(jnp.mean(x1 * x1, axis=-1, keepdims=True) + EPS) * pf_ref[...]
        h2_ref[...] = h2.astype(BF16)


def mix_out(o_hgrn, z, o_att, x, w_out, hgrn_out_norm, attn_out_norm, norm_post_mix, norm_pre_ffn,
            tm, tn):
    m, d = x.shape
    kern = functools.partial(_mix_out_kernel, tn=tn)
    gsec = ZG * HEAD_DIM // W_GROUP
    vec = lambda n: pl.BlockSpec((1, n), lambda i, j: (0, 0))
    return pl.pallas_call(
        kern,
        out_shape=(jax.ShapeDtypeStruct((m, d), F32), jax.ShapeDtypeStruct((m, d), BF16)),
        grid=(m // tm, d // tn),
        in_specs=[pl.BlockSpec((tm, W_GROUP), lambda i, j: (i, 0)),
                  pl.BlockSpec((tm, W_GROUP), lambda i, j: (i, gsec)),
                  pl.BlockSpec((tm, W_GROUP), lambda i, j: (i, 0)),
                  pl.BlockSpec((tm, d), lambda i, j: (i, 0)),
                  pl.BlockSpec((2 * W_GROUP, tn), lambda i, j: (0, j)),
                  vec(HEAD_DIM), vec(W_GROUP), vec(d), vec(d)],
        out_specs=(pl.BlockSpec((tm, d), lambda i, j: (i, 0)),
                   pl.BlockSpec((tm, d), lambda i, j: (i, 0))),
        scratch_shapes=[pltpu.VMEM((tm, 2 * W_GROUP), BF16), pltpu.VMEM((tm, d), F32)],
        compiler_params=_cparams(("arbitrary", "arbitrary")),
        name="mix_out",
    )(o_hgrn, z, o_att, x, w_out, hgrn_out_norm.reshape(1, -1), attn_out_norm.reshape(1, -1),
      norm_post_mix.reshape(1, -1), norm_pre_ffn.reshape(1, -1))


def _ffn_kernel(h_ref, wg_ref, wu_ref, wd_ref, x1_ref, pn_ref, y_ref):
    f = pl.program_id(1)
    nf = pl.num_programs(1)

    @pl.when(f == 0)
    def _():
        y_ref[...] = jnp.zeros(y_ref.shape, F32)

    h = h_ref[...]
    g = jnp.dot(h, wg_ref[...], preferred_element_type=F32)
    u = jnp.dot(h, wu_ref[...], preferred_element_type=F32)
    a = (g * _sigmoid(g) * u).astype(BF16)
    y_ref[...] += jnp.dot(a, wd_ref[...], preferred_element_type=F32)

    @pl.when(f == nf - 1)
    def _():
        ff = y_ref[...]
        y_ref[...] = x1_ref[...] + ff * lax.rsqrt(jnp.mean(ff * ff, axis=-1, keepdims=True) + EPS) * pn_ref[...]


def ffn(h2, x1, w_gate, w_up, w_down, norm_post_ffn, tm, tf):
    m, d = h2.shape
    dff = w_gate.shape[1]
    once = pl.Buffered(1)
    return pl.pallas_call(
        _ffn_kernel,
        out_shape=jax.ShapeDtypeStruct((m, d), F32),
        grid=(m // tm, dff // tf),
        in_specs=[pl.BlockSpec((tm, d), lambda i, f: (i, 0), pipeline_mode=once),
                  pl.BlockSpec((d, tf), lambda i, f: (0, f)),
                  pl.BlockSpec((d, tf), lambda i, f: (0, f)),
                  pl.BlockSpec((tf, d), lambda i, f: (f, 0)),
                  pl.BlockSpec((tm, d), lambda i, f: (i, 0), pipeline_mode=once),
                  pl.BlockSpec((1, d), lambda i, f: (0, 0))],
        out_specs=pl.BlockSpec((tm, d), lambda i, f: (i, 0)),
        compiler_params=_cparams(("arbitrary", "arbitrary")),
        name="ffn",
    )(h2, w_gate, w_up, w_down, x1, norm_post_ffn.reshape(1, -1))


def _alibi_slopes(n):
    return 2.0 ** (-8.0 * jnp.arange(1, n + 1, dtype=F32) / n)


def _token_tile(m, cap):
    return cap if m % cap == 0 else m


def _project(x2d, norm_pre_mix, w_in16):
    m = x2d.shape[0]
    h = rmsnorm_bf16(x2d, norm_pre_mix, _token_tile(m, 256))
    tm = _token_tile(m, 1024)
    z = matmul_cols(h, w_in16, 0, 5 * W_GROUP, tm, 1024)
    k = matmul_cols(h, w_in16, 5 * W_GROUP, W_GROUP, tm, 1024)
    v = matmul_cols(h, w_in16, 6 * W_GROUP, W_GROUP, tm, 1024)
    return z, k, v


def _finish(x2d, o_hgrn, z, o_att, w16, norms):
    hgrn_out_norm, attn_out_norm, norm_post_mix, norm_pre_ffn, norm_post_ffn = norms
    w_out16, w_gate16, w_up16, w_down16 = w16
    m = x2d.shape[0]
    x1, h2 = mix_out(o_hgrn, z, o_att, x2d, w_out16, hgrn_out_norm, attn_out_norm, norm_post_mix,
                     norm_pre_ffn, _token_tile(m, 256), 512)
    return ffn(h2, x1, w_gate16, w_up16, w_down16, norm_post_ffn, _token_tile(m, 512), 256)


def kernel(x_prompt, x_sample, cache_k, cache_v, page_table, state_hgrn, norm_pre_mix, w_in,
           hgrn_lb_logits, hgrn_out_norm, attn_out_norm, w_out, norm_post_mix, norm_pre_ffn,
           w_gate, w_up, w_down, norm_post_ffn):
    bp, tp, d = x_prompt.shape
    bs, ts, _ = x_sample.shape
    assert bp == 1
    n_pool, page, h_att, hd = cache_k.shape
    past_len = page_table.shape[1] * page
    slopes = _alibi_slopes(N_ATT_HEADS)
    w_in16, w_out16, w_gate16, w_up16, w_down16 = (
        w.astype(BF16) for w in (w_in, w_out, w_gate, w_up, w_down))
    w16 = (w_out16, w_gate16, w_up16, w_down16)
    norms = (hgrn_out_norm, attn_out_norm, norm_post_mix, norm_pre_ffn, norm_post_ffn)

    xp = x_prompt.reshape(tp, d)
    zp, kp, vp = _project(xp, norm_pre_mix, w_in16)
    s0 = jnp.zeros((bp,) + state_hgrn.shape[1:], state_hgrn.dtype)
    oh_p, state_prompt = hgrn2(zp, hgrn_lb_logits, s0, HEAD_DIM, HEAD_DIM)
    oa_p = moba_prompt(zp, kp, vp, slopes)
    y_prompt = _finish(xp, oh_p, zp, oa_p, w16, norms)

    xs = x_sample.reshape(bs * ts, d)
    zs, ks, vs = _project(xs, norm_pre_mix, w_in16)
    oh_s, state_sample = hgrn2(zs, hgrn_lb_logits, state_hgrn, ts, HGRN_SUB)
    oa_s = moba_sample(zs, ks.reshape(bs * ts * h_att, hd), vs.reshape(bs * ts * h_att, hd),
                       cache_k.reshape(n_pool, page * h_att, hd),
                       cache_v.reshape(n_pool, page * h_att, hd), page_table, slopes, past_len)
    y_sample = _finish(xs, oh_s, zs, oa_s, w16, norms)

    return (y_prompt.reshape(bp, tp, d), y_sample.reshape(bs, ts, d),
            kp.reshape(bp, tp, h_att, hd), vp.reshape(bp, tp, h_att, hd),
            ks.reshape(bs, ts, h_att, hd), vs.reshape(bs, ts, h_att, hd),
            state_prompt, state_sample)
```

```python
import functools

import jax
import jax.numpy as jnp
from jax import lax
from jax.experimental import pallas as pl
from jax.experimental.pallas import tpu as pltpu

F32 = jnp.float32
BF16 = jnp.bfloat16

EPS = 1e-6
HEAD_DIM = 128
N_HGRN_HEADS = 16
N_ATT_HEADS = 16
W_GROUP = N_HGRN_HEADS * HEAD_DIM
MOBA_BLOCK = 256
MOBA_TOPK = 3
PAGE_SIZE = 128
PAGES_PER_BLOCK = MOBA_BLOCK // PAGE_SIZE
ATT_SCALE = HEAD_DIM ** -0.5
HGRN_SUB = 16
NEG_INF = float("-inf")
LOG2E = 1.4426950408889634
MOBA_QB = 2
MOBA_GROUP = 8
VMEM_LIMIT = 56 * 1024 * 1024

ZQ, ZF, ZI, ZG, ZA = 0, 16, 32, 48, 64


def _cparams(sem):
    return pltpu.CompilerParams(dimension_semantics=sem, vmem_limit_bytes=VMEM_LIMIT)


def _nt(a, b, precision=None):
    return lax.dot_general(a, b, (((1,), (1,)), ((), ())), preferred_element_type=F32,
                           precision=precision)


def _sigmoid(x):
    return 1.0 / (1.0 + jnp.exp(-x))


def _norm_kernel(x_ref, g_ref, o_ref):
    x = x_ref[...]
    y = x * lax.rsqrt(jnp.mean(x * x, axis=-1, keepdims=True) + EPS)
    o_ref[...] = (y * g_ref[...]).astype(BF16)


def rmsnorm_bf16(x, g, tm):
    m, d = x.shape
    return pl.pallas_call(
        _norm_kernel,
        out_shape=jax.ShapeDtypeStruct((m, d), BF16),
        grid=(m // tm,),
        in_specs=[pl.BlockSpec((tm, d), lambda i: (i, 0)), pl.BlockSpec((1, d), lambda i: (0, 0))],
        out_specs=pl.BlockSpec((tm, d), lambda i: (i, 0)),
        compiler_params=_cparams(("arbitrary",)),
        name="rmsnorm_bf16",
    )(x, g.reshape(1, d))


def _mm_kernel(a_ref, w_ref, o_ref):
    o_ref[...] = jnp.dot(a_ref[...], w_ref[...], preferred_element_type=F32)


def matmul_cols(a, w, col0, n, tm, tn):
    m, k = a.shape
    cb = col0 // tn
    return pl.pallas_call(
        _mm_kernel,
        out_shape=jax.ShapeDtypeStruct((m, n), F32),
        grid=(m // tm, n // tn),
        in_specs=[pl.BlockSpec((tm, k), lambda i, j: (i, 0)),
                  pl.BlockSpec((k, tn), lambda i, j: (0, cb + j))],
        out_specs=pl.BlockSpec((tm, tn), lambda i, j: (i, j)),
        compiler_params=_cparams(("arbitrary", "arbitrary")),
        name="matmul_cols",
    )(a, w)


def _cumsum_rows(g):
    c = g.shape[0]
    row = lax.broadcasted_iota(jnp.int32, g.shape, 0)
    b = g
    sh = 1
    while sh < c:
        b = b + jnp.where(row >= sh, pltpu.roll(b, sh, 0), 0.0)
        sh *= 2
    return b


def _hgrn_kernel(q_ref, f_ref, i_ref, lbl_ref, s0_ref, o_ref, sout_ref, st_ref, *, rows_in, rows, hp):
    c = pl.program_id(2)
    nc = pl.num_programs(2)
    for hh in range(hp):
        cols = slice(hh * HEAD_DIM, (hh + 1) * HEAD_DIM)

        @pl.when(c == 0)
        def _(hh=hh):
            st_ref[hh] = s0_ref[0, hh].T

        o, st_new = _hgrn_chunk(q_ref[:, cols], f_ref[:, cols], i_ref[:, cols], lbl_ref[:, cols],
                                st_ref[hh], rows_in, rows)
        o_ref[:, cols] = o
        st_ref[hh] = st_new

        @pl.when(c == nc - 1)
        def _(hh=hh, st_new=st_new):
            sout_ref[0, hh] = st_new.T


def _hgrn_chunk(hq, hf, v, lg, st, rows_in, rows):
    lge = jnp.exp(lg - jnp.max(lg, axis=0, keepdims=True))
    lb = lge[0:1, :] / jnp.sum(lge, axis=0, keepdims=True)

    f = lb + (1.0 - lb) * _sigmoid(hf)
    q = hq * _sigmoid(hq)
    k = 1.0 - f
    g = jnp.log(f)
    if rows > rows_in:
        pad = jnp.zeros((rows - rows_in, HEAD_DIM), F32)
        q, k, g, v = (jnp.concatenate([a, pad], axis=0) for a in (q, k, g, v))
    b = _cumsum_rows(g)

    m = HGRN_SUB
    ns = rows // m

    o = _nt((q * jnp.exp(b)).astype(BF16), st.astype(BF16))

    def exact_diag(width):
        row_w = lax.broadcasted_iota(jnp.int32, (width, HEAD_DIM), 0)
        outs = []
        for i in range(rows // width):
            sl = slice(i * width, (i + 1) * width)
            qi, ki, bi, vi = q[sl], k[sl], b[sl], v[sl]
            oi = jnp.zeros((width, HEAD_DIM), F32)
            for s in range(width):
                e = jnp.exp(jnp.where(row_w >= s, bi - bi[s:s + 1], NEG_INF))
                a = jnp.sum(qi * e * ki[s:s + 1], axis=-1, keepdims=True)
                oi = oi + a * vi[s:s + 1]
            outs.append(oi)
        return outs

    half = m // 2
    if ns >= 2:
        o_diag = exact_diag(half)
        q3, k3, b3, v3 = (a.reshape(ns, m, HEAD_DIM) for a in (q, k, b, v))
        r = b3[:, half - 1:half, :]
        qt = (q3[:, half:, :] * jnp.exp(b3[:, half:, :] - r)).reshape(ns * half, HEAD_DIM)
        kt = (k3[:, :half, :] * jnp.exp(r - b3[:, :half, :])).reshape(ns * half, HEAD_DIM)
        a = _nt(qt.astype(BF16), kt.astype(BF16))
        arow = lax.broadcasted_iota(jnp.int32, a.shape, 0) // half
        acol = lax.broadcasted_iota(jnp.int32, a.shape, 1) // half
        a = jnp.where(arow == acol, a, 0.0)
        low = jnp.dot(a.astype(BF16), v3[:, :half, :].reshape(ns * half, HEAD_DIM).astype(BF16),
                      preferred_element_type=F32)
        for i in range(ns):
            o_diag[2 * i + 1] = o_diag[2 * i + 1] + low[i * half:(i + 1) * half]
    else:
        o_diag = exact_diag(m)
    o = o + jnp.concatenate(o_diag, axis=0) if len(o_diag) > 1 else o + o_diag[0]

    for j in range(ns - 1):
        lo = (j + 1) * m
        r = b[lo - 1:lo]
        kt = k[j * m:lo] * jnp.exp(r - b[j * m:lo])
        qt = q[lo:] * jnp.exp(b[lo:] - r)
        a = _nt(qt.astype(BF16), kt.astype(BF16))
        contrib = jnp.dot(a.astype(BF16), v[j * m:lo].astype(BF16), preferred_element_type=F32)
        o = o + jnp.concatenate([jnp.zeros((lo, HEAD_DIM), F32), contrib], axis=0)

    b_last = b[rows - 1:rows]
    kdec = k * jnp.exp(b_last - b)
    vp, kp = v, kdec
    if rows < HEAD_DIM:
        zp = jnp.zeros((HEAD_DIM - rows, HEAD_DIM), F32)
        vp, kp = jnp.concatenate([v, zp], axis=0), jnp.concatenate([kdec, zp], axis=0)
    st_new = st * jnp.exp(b_last) + jnp.dot(vp.T.astype(BF16), kp.astype(BF16),
                                             preferred_element_type=F32)
    return o[:rows_in], st_new


def hgrn2(z, lb_logits, s0, rows_in, rows, hp):
    bsz = s0.shape[0]
    t = z.shape[0] // bsz
    nc = t // rows_in
    kern = functools.partial(_hgrn_kernel, rows_in=rows_in, rows=rows, hp=hp)
    wide = hp * HEAD_DIM

    def zspec(off):
        return pl.BlockSpec((rows_in, wide), lambda b, h, c: (b * nc + c, off // hp + h))

    sspec = pl.BlockSpec((1, hp, HEAD_DIM, HEAD_DIM), lambda b, h, c: (b, h, 0, 0))
    return pl.pallas_call(
        kern,
        out_shape=(jax.ShapeDtypeStruct((bsz * t, W_GROUP), F32),
                   jax.ShapeDtypeStruct(s0.shape, s0.dtype)),
        grid=(bsz, N_HGRN_HEADS // hp, nc),
        in_specs=[zspec(ZQ), zspec(ZF), zspec(ZI),
                  pl.BlockSpec((lb_logits.shape[0], wide), lambda b, h, c: (0, h)),
                  sspec],
        out_specs=(pl.BlockSpec((rows_in, wide), lambda b, h, c: (b * nc + c, h)), sspec),
        scratch_shapes=[pltpu.VMEM((hp, HEAD_DIM, HEAD_DIM), F32)],
        compiler_params=_cparams(("arbitrary", "arbitrary", "arbitrary")),
        name="hgrn2",
    )(z, z, z, lb_logits, s0)


def _topk_rows_mask(g, topk):
    nb = g.shape[0]
    jj = lax.broadcasted_iota(jnp.int32, g.shape, 0)
    sel = jnp.zeros(g.shape, F32)
    for _ in range(topk):
        mx = jnp.max(g, axis=0, keepdims=True)
        idx = jnp.min(jnp.where(g == mx, jj, nb), axis=0, keepdims=True)
        hit = jj == idx
        sel = jnp.where(jnp.logical_and(hit, mx > NEG_INF), 1.0, sel)
        g = jnp.where(hit, NEG_INF, g)
    return sel


def _split3_bf16(x):
    a = x.astype(BF16)
    r = x - a.astype(F32)
    b = r.astype(BF16)
    return a, b, (r - b.astype(F32)).astype(BF16)


def _moba_prompt_kernel(slope_ref, q_ref, k_ref, v_ref, o_ref,
                        ka_ref, vt_ref, kmean_ref, qa_ref, sel_ref, *, nb, topk, group):
    blk = MOBA_BLOCK
    qw = MOBA_QB * blk
    h = pl.program_id(0)
    qi = pl.program_id(1)
    slope2 = slope_ref[h] * LOG2E

    @pl.when(qi == 0)
    def _():
        lane = lax.broadcasted_iota(jnp.int32, (blk, HEAD_DIM), 1)
        rloc = lax.broadcasted_iota(jnp.int32, (blk, HEAD_DIM), 0).astype(F32)
        b1, b2, b3 = _split3_bf16(slope2 * rloc)
        aug = jnp.where(lane == 0, b1.astype(F32),
                        jnp.where(lane == 1, b2.astype(F32),
                                  jnp.where(lane == 2, b3.astype(F32), 0.0))).astype(BF16)

        def prep(j, carry):
            r0 = pl.multiple_of(j * blk, blk)
            kj = k_ref[pl.ds(r0, blk), :]
            ka_ref[pl.ds(r0, blk), :] = jnp.concatenate([kj.astype(BF16), aug], axis=1)
            vt_ref[:, pl.ds(r0, blk)] = v_ref[pl.ds(r0, blk), :].T.astype(BF16)
            rowj = lax.broadcasted_iota(jnp.int32, (nb, HEAD_DIM), 0)
            kmean_ref[...] = jnp.where(rowj == j, jnp.mean(kj, axis=0, keepdims=True), kmean_ref[...])
            return carry
        lax.fori_loop(0, nb, prep, 0)

    q = q_ref[...] * ATT_SCALE
    lane_q = lax.broadcasted_iota(jnp.int32, (qw, HEAD_DIM), 1)
    ones = jnp.where(lane_q < 3, 1.0, 0.0).astype(BF16)
    qa_ref[...] = jnp.concatenate([(q * LOG2E).astype(BF16), ones], axis=1)

    gate = _nt(kmean_ref[...], q, precision=lax.Precision.HIGHEST)
    jj = lax.broadcasted_iota(jnp.int32, (nb, qw), 0)
    qb_lane = MOBA_QB * qi + lax.broadcasted_iota(jnp.int32, (nb, qw), 1) // blk
    sel_ref[...] = _topk_rows_mask(jnp.where(jj < qb_lane, gate, NEG_INF), topk)

    r0 = pl.multiple_of(qi * qw, qw)
    tt = lax.broadcasted_iota(jnp.int32, (qw, qw), 1)
    rr = lax.broadcasted_iota(jnp.int32, (qw, qw), 0)
    own = jnp.logical_and(tt >= rr, tt // blk == rr // blk)
    s = jnp.where(own, _nt(ka_ref[pl.ds(r0, qw), :], qa_ref[...]), NEG_INF)
    cq = slope2 * (qb_lane[0:1, :] * blk).astype(F32)
    smax = jnp.max(s, axis=0, keepdims=True)
    p = jnp.exp2(s - smax)
    m0 = smax + cq
    l0 = jnp.sum(p, axis=0, keepdims=True)
    acc0 = jnp.dot(vt_ref[:, pl.ds(r0, qw)], p.astype(BF16), preferred_element_type=F32)

    def body(g, carry):
        m, l, acc = carry
        j0 = g * group
        c0 = pl.multiple_of(j0 * blk, group * blk)
        selg = sel_ref[pl.ds(pl.multiple_of(j0, group), group), :]
        s = _nt(ka_ref[pl.ds(c0, group * blk), :], qa_ref[...])
        ons, cjs = [], []
        mn = m
        for b in range(group):
            cj = slope2 * ((j0 + b) * blk).astype(F32)
            on = selg[b:b + 1, :] > 0.0
            cm = jnp.max(s[b * blk:(b + 1) * blk], axis=0, keepdims=True) + cj
            mn = jnp.maximum(mn, jnp.where(on, cm, NEG_INF))
            ons.append(on)
            cjs.append(cj)
        ps = [jnp.exp2(s[b * blk:(b + 1) * blk] - jnp.where(ons[b], mn - cjs[b], float("inf")))
              for b in range(group)]
        alpha = jnp.exp2(m - mn)
        lsum = ps[0].sum(axis=0, keepdims=True)
        for b in range(1, group):
            lsum = lsum + ps[b].sum(axis=0, keepdims=True)
        p = jnp.concatenate([x.astype(BF16) for x in ps], axis=0)
        pv = jnp.dot(vt_ref[:, pl.ds(c0, group * blk)], p, preferred_element_type=F32)
        return mn, l * alpha + lsum, acc * alpha + pv

    n_groups = (MOBA_QB * qi + MOBA_QB - 1 + group - 1) // group
    _, l, acc = lax.fori_loop(0, n_groups, body, (m0, l0, acc0))
    o_ref[...] = (acc / l).T


def moba_prompt(z, k, v, slopes):
    t = k.shape[0]
    nb = t // MOBA_BLOCK
    topk = min(MOBA_TOPK, nb - 1)
    group = next(g for g in (MOBA_GROUP, 4, 2, 1) if nb % g == 0)
    qw = MOBA_QB * MOBA_BLOCK
    kern = functools.partial(_moba_prompt_kernel, nb=nb, topk=topk, group=group)
    grid_spec = pltpu.PrefetchScalarGridSpec(
        num_scalar_prefetch=1,
        grid=(N_ATT_HEADS, nb // MOBA_QB),
        in_specs=[pl.BlockSpec((qw, HEAD_DIM), lambda h, qi, sl: (qi, ZA + h)),
                  pl.BlockSpec((t, HEAD_DIM), lambda h, qi, sl: (0, h)),
                  pl.BlockSpec((t, HEAD_DIM), lambda h, qi, sl: (0, h))],
        out_specs=pl.BlockSpec((qw, HEAD_DIM), lambda h, qi, sl: (qi, h)),
        scratch_shapes=[pltpu.VMEM((t, 2 * HEAD_DIM), BF16),
                        pltpu.VMEM((HEAD_DIM, t), BF16),
                        pltpu.VMEM((nb, HEAD_DIM), F32),
                        pltpu.VMEM((qw, 2 * HEAD_DIM), BF16),
                        pltpu.VMEM((nb, qw), F32)],
    )
    return pl.pallas_call(
        kern,
        out_shape=jax.ShapeDtypeStruct((t, N_ATT_HEADS * HEAD_DIM), F32),
        grid_spec=grid_spec,
        compiler_params=_cparams(("arbitrary", "arbitrary")),
        name="moba_prompt",
    )(slopes, z, k, v)


def _lane_topk_mask(g, topk, n_valid):
    lane = lax.broadcasted_iota(jnp.int32, g.shape, 1).astype(F32)
    g = jnp.where(lane < n_valid, g, NEG_INF)
    sel = jnp.zeros(g.shape, F32)
    for _ in range(topk):
        mx = jnp.max(g, axis=1, keepdims=True)
        idx = jnp.min(jnp.where(g == mx, lane, float(g.shape[1])), axis=1, keepdims=True)
        hit = lane == idx
        sel = jnp.where(jnp.logical_and(hit, mx > NEG_INF), 1.0, sel)
        g = jnp.where(hit, NEG_INF, g)
    return sel


def _moba_sample_kernel(pt_ref, slope_ref, q_ref, kn_ref, vn_ref, ck0_ref, ck1_ref, cv0_ref, cv1_ref,
                        o_ref, qf_ref, q16_ref, bias_ref, ks_ref, m_ref, l_ref, acc_ref,
                        *, nb, nt, past_len):
    del pt_ref
    blk = MOBA_BLOCK
    nh = N_ATT_HEADS
    hh = nh // 2
    nl = nh * nt
    nlh = hh * nt
    wide = blk * hh
    j = pl.program_id(1)
    slope2 = slope_ref[...] * LOG2E

    @pl.when(j == 0)
    def _():
        aq = q_ref[...] * ATT_SCALE
        qf = jnp.concatenate([aq[:, h * HEAD_DIM:(h + 1) * HEAD_DIM] for h in range(nh)], axis=0)
        qf_ref[...] = qf
        q16_ref[...] = (qf * LOG2E).astype(BF16)
        qrow = lax.broadcasted_iota(jnp.int32, (nl, wide), 0)
        kcol = lax.broadcasted_iota(jnp.int32, (nl, wide), 1)
        same_head = kcol % hh == (qrow // nt) % hh
        bias_ref[...] = jnp.where(same_head, -slope2 * (qrow % nt - kcol // hh).astype(F32), NEG_INF)
        m_ref[...] = jnp.zeros(m_ref.shape, F32)
        l_ref[...] = jnp.zeros(l_ref.shape, F32)

    mjs, ljs, accs = [], [], []
    for g in range(2):
        rows = slice(g * nlh, (g + 1) * nlh)
        kg = jnp.concatenate([ck0_ref[0, :, g].reshape(PAGE_SIZE * hh, HEAD_DIM),
                              ck1_ref[0, :, g].reshape(PAGE_SIZE * hh, HEAD_DIM)], axis=0)
        vg = jnp.concatenate([cv0_ref[0, :, g].reshape(PAGE_SIZE * hh, HEAD_DIM),
                              cv1_ref[0, :, g].reshape(PAGE_SIZE * hh, HEAD_DIM)], axis=0)
        ks_ref[pl.ds(pl.multiple_of(j * nh + g * hh, hh), hh), :] = jnp.sum(
            kg.reshape(blk, hh, HEAD_DIM), axis=0)
        s = _nt(q16_ref[rows, :], kg.astype(BF16)) + bias_ref[rows, :]
        mj = jnp.max(s, axis=1, keepdims=True)
        p = jnp.exp2(s - mj)
        mjs.append(mj)
        ljs.append(jnp.sum(p, axis=1, keepdims=True))
        accs.append(jnp.dot(p.astype(BF16), vg.astype(BF16), preferred_element_type=F32))
    mj = jnp.concatenate(mjs, axis=0)
    acc_ref[j] = jnp.concatenate(accs, axis=0)
    lane = lax.broadcasted_iota(jnp.int32, (nl, HEAD_DIM), 1)
    off = (past_len - j * blk).astype(F32)
    m_ref[...] = jnp.where(lane == j, mj - slope2 * off, m_ref[...])
    l_ref[...] = jnp.where(lane == j, jnp.concatenate(ljs, axis=0), l_ref[...])

    @pl.when(j == nb - 1)
    def _():
        qf = qf_ref[...]
        qrow = lax.broadcasted_iota(jnp.int32, (nl, nb * nh), 0)
        gcol = lax.broadcasted_iota(jnp.int32, (nl, nb * nh), 1)
        g_all = _nt(qf, ks_ref[...] * (1.0 / blk), precision=lax.Precision.HIGHEST)
        g_own = jnp.where(gcol % nh == qrow // nt, g_all, 0.0)
        erow = lax.broadcasted_iota(jnp.int32, (nb * nh, HEAD_DIM), 0)
        ecol = lax.broadcasted_iota(jnp.int32, (nb * nh, HEAD_DIM), 1)
        pick = jnp.where(erow // nh == ecol, 1.0, 0.0)
        gate = jnp.dot(g_own, pick, preferred_element_type=F32, precision=lax.Precision.HIGHEST)
        w = _lane_topk_mask(gate, min(MOBA_TOPK, nb), nb)

        orow = lax.broadcasted_iota(jnp.int32, (nl, nl), 0)
        ocol = lax.broadcasted_iota(jnp.int32, (nl, nl), 1)
        dist = orow % nt - ocol // nh
        ok = jnp.logical_and(ocol % nh == orow // nt, dist >= 0)
        so = jnp.where(ok, _nt(q16_ref[...], kn_ref[...].astype(BF16)) - slope2 * dist.astype(F32),
                       NEG_INF)
        mo = jnp.max(so, axis=1, keepdims=True)

        mall = m_ref[...]
        mtot = jnp.maximum(mo, jnp.max(jnp.where(w > 0.0, mall, NEG_INF), axis=1, keepdims=True))
        wj = jnp.where(w > 0.0, jnp.exp2(mall - mtot), 0.0)
        po = jnp.exp2(so - mtot)
        denom = jnp.sum(wj * l_ref[...], axis=1, keepdims=True) + jnp.sum(po, axis=1, keepdims=True)
        num = jnp.dot(po.astype(BF16), vn_ref[...].astype(BF16), preferred_element_type=F32)
        for jb in range(nb):
            num = num + wj[:, jb:jb + 1] * acc_ref[jb]
        out = num / denom
        for h in range(nh):
            o_ref[:, h * HEAD_DIM:(h + 1) * HEAD_DIM] = out[h * nt:(h + 1) * nt, :]


def moba_sample(z, k_new, v_new, cache_k, cache_v, page_table, slopes, past_len):
    bsz, n_pages = page_table.shape
    nh = N_ATT_HEADS
    nt = k_new.shape[0] // (bsz * nh)
    nb = past_len // MOBA_BLOCK
    assert n_pages == nb * PAGES_PER_BLOCK and PAGES_PER_BLOCK == 2
    assert nh * nt == HEAD_DIM and nb <= HEAD_DIM
    wd = nh * HEAD_DIM
    slope_col = jnp.repeat(slopes, nt).reshape(nh * nt, 1)
    kern = functools.partial(_moba_sample_kernel, nb=nb, nt=nt, past_len=past_len)

    def page_spec(which):
        return pl.BlockSpec((1, PAGE_SIZE, 2, nh // 2, HEAD_DIM),
                            lambda b, j, pt: (pt[b * n_pages + 2 * j + which], 0, 0, 0, 0))

    new_spec = pl.BlockSpec((nt * nh, HEAD_DIM), lambda b, j, pt: (b, 0))
    grid_spec = pltpu.PrefetchScalarGridSpec(
        num_scalar_prefetch=1,
        grid=(bsz, nb),
        in_specs=[pl.BlockSpec((nh * nt, 1), lambda b, j, pt: (0, 0)),
                  pl.BlockSpec((nt, wd), lambda b, j, pt: (b, ZA * HEAD_DIM // wd)),
                  new_spec, new_spec,
                  page_spec(0), page_spec(1), page_spec(0), page_spec(1)],
        out_specs=pl.BlockSpec((nt, wd), lambda b, j, pt: (b, 0)),
        scratch_shapes=[pltpu.VMEM((nh * nt, HEAD_DIM), F32),
                        pltpu.VMEM((nh * nt, HEAD_DIM), BF16),
                        pltpu.VMEM((nh * nt, MOBA_BLOCK * nh // 2), F32),
                        pltpu.VMEM((nb * nh, HEAD_DIM), F32),
                        pltpu.VMEM((nh * nt, HEAD_DIM), F32),
                        pltpu.VMEM((nh * nt, HEAD_DIM), F32),
                        pltpu.VMEM((nb, nh * nt, HEAD_DIM), F32)],
    )
    return pl.pallas_call(
        kern,
        out_shape=jax.ShapeDtypeStruct((bsz * nt, wd), F32),
        grid_spec=grid_spec,
        compiler_params=_cparams(("arbitrary", "arbitrary")),
        name="moba_sample",
    )(page_table.reshape(-1), slope_col, z, k_new, v_new, cache_k, cache_k, cache_v, cache_v)


def _mix_out_kernel(oh_ref, g_ref, oa_ref, x_ref, w_ref, hn_ref, an_ref, pm_ref, pf_ref,
                    x1_ref, h2_ref, *, tn):
    j = pl.program_id(1)
    nj = pl.num_programs(1)

    @pl.when(j == 0)
    def _():
        hn = hn_ref[...]
        for h in range(N_HGRN_HEADS):
            sl = slice(h * HEAD_DIM, (h + 1) * HEAD_DIM)
            o = oh_ref[:, sl]
            y = o * lax.rsqrt(jnp.mean(o * o, axis=-1, keepdims=True) + EPS) * hn
            gh = g_ref[:, sl]
            h2_ref[:, sl] = (y * (gh * _sigmoid(gh))).astype(BF16)
        oa = oa_ref[...]
        ya = oa * lax.rsqrt(jnp.mean(oa * oa, axis=-1, keepdims=True) + EPS) * an_ref[...]
        h2_ref[:, W_GROUP:] = ya.astype(BF16)

    c0 = pl.multiple_of(j * tn, tn)
    x1_ref[:, pl.ds(c0, tn)] = jnp.dot(h2_ref[...], w_ref[...], preferred_element_type=F32)

    @pl.when(j == nj - 1)
    def _():
        mix = x1_ref[...]
        x1 = x_ref[...] + mix * lax.rsqrt(jnp.mean(mix * mix, axis=-1, keepdims=True) + EPS) * pm_ref[...]
        x1_ref[...] = x1
        h2 = x1 * lax.rsqrt(jnp.mean(x1 * x1, axis=-1, keepdims=True) + EPS) * pf_ref[...]
        h2_ref[...] = h2.astype(BF16)


def mix_out(o_hgrn, z, o_att, x, w_out, hgrn_out_norm, attn_out_norm, norm_post_mix, norm_pre_ffn,
            tm, tn):
    m, d = x.shape
    kern = functools.partial(_mix_out_kernel, tn=tn)
    gsec = ZG * HEAD_DIM // W_GROUP
    vec = lambda n: pl.BlockSpec((1, n), lambda i, j: (0, 0))
    once = pl.Buffered(1)
    return pl.pallas_call(
        kern,
        out_shape=(jax.ShapeDtypeStruct((m, d), F32), jax.ShapeDtypeStruct((m, d), BF16)),
        grid=(m // tm, d // tn),
        in_specs=[pl.BlockSpec((tm, W_GROUP), lambda i, j: (i, 0), pipeline_mode=once),
                  pl.BlockSpec((tm, W_GROUP), lambda i, j: (i, gsec), pipeline_mode=once),
                  pl.BlockSpec((tm, W_GROUP), lambda i, j: (i, 0), pipeline_mode=once),
                  pl.BlockSpec((tm, d), lambda i, j: (i, 0), pipeline_mode=once),
                  pl.BlockSpec((2 * W_GROUP, tn), lambda i, j: (0, j)),
                  vec(HEAD_DIM), vec(W_GROUP), vec(d), vec(d)],
        out_specs=(pl.BlockSpec((tm, d), lambda i, j: (i, 0)),
                   pl.BlockSpec((tm, d), lambda i, j: (i, 0))),
        compiler_params=_cparams(("arbitrary", "arbitrary")),
        name="mix_out",
    )(o_hgrn, z, o_att, x, w_out, hgrn_out_norm.reshape(1, -1), attn_out_norm.reshape(1, -1),
      norm_post_mix.reshape(1, -1), norm_pre_ffn.reshape(1, -1))


def _ffn_kernel(h_ref, wg_ref, wu_ref, wd_ref, x1_ref, pn_ref, y_ref):
    f = pl.program_id(1)
    nf = pl.num_programs(1)

    @pl.when(f == 0)
    def _():
        y_ref[...] = jnp.zeros(y_ref.shape, F32)

    h = h_ref[...]
    g = jnp.dot(h, wg_ref[...], preferred_element_type=F32)
    u = jnp.dot(h, wu_ref[...], preferred_element_type=F32)
    a = (g * _sigmoid(g) * u).astype(BF16)
    y_ref[...] += jnp.dot(a, wd_ref[...], preferred_element_type=F32)

    @pl.when(f == nf - 1)
    def _():
        ff = y_ref[...]
        y_ref[...] = x1_ref[...] + ff * lax.rsqrt(jnp.mean(ff * ff, axis=-1, keepdims=True) + EPS) * pn_ref[...]


def ffn(h2, x1, w_gate, w_up, w_down, norm_post_ffn, tm, tf):
    m, d = h2.shape
    dff = w_gate.shape[1]
    once = pl.Buffered(1)
    return pl.pallas_call(
        _ffn_kernel,
        out_shape=jax.ShapeDtypeStruct((m, d), F32),
        grid=(m // tm, dff // tf),
        in_specs=[pl.BlockSpec((tm, d), lambda i, f: (i, 0), pipeline_mode=once),
                  pl.BlockSpec((d, tf), lambda i, f: (0, f)),
                  pl.BlockSpec((d, tf), lambda i, f: (0, f)),
                  pl.BlockSpec((tf, d), lambda i, f: (f, 0)),
                  pl.BlockSpec((tm, d), lambda i, f: (i, 0), pipeline_mode=once),
                  pl.BlockSpec((1, d), lambda i, f: (0, 0))],
        out_specs=pl.BlockSpec((tm, d), lambda i, f: (i, 0)),
        compiler_params=_cparams(("arbitrary", "arbitrary")),
        name="ffn",
    )(h2, w_gate, w_up, w_down, x1, norm_post_ffn.reshape(1, -1))


def _alibi_slopes(n):
    return 2.0 ** (-8.0 * jnp.arange(1, n + 1, dtype=F32) / n)


def _token_tile(m, cap):
    return cap if m % cap == 0 else m


def _project(x2d, norm_pre_mix, w_in16):
    m = x2d.shape[0]
    h = rmsnorm_bf16(x2d, norm_pre_mix, _token_tile(m, 256))
    tm = _token_tile(m, 1024)
    z = matmul_cols(h, w_in16, 0, 5 * W_GROUP, tm, 1024)
    k = matmul_cols(h, w_in16, 5 * W_GROUP, W_GROUP, tm, 1024)
    v = matmul_cols(h, w_in16, 6 * W_GROUP, W_GROUP, tm, 1024)
    return z, k, v


def _finish(x2d, o_hgrn, z, o_att, w16, norms):
    hgrn_out_norm, attn_out_norm, norm_post_mix, norm_pre_ffn, norm_post_ffn = norms
    w_out16, w_gate16, w_up16, w_down16 = w16
    m = x2d.shape[0]
    x1, h2 = mix_out(o_hgrn, z, o_att, x2d, w_out16, hgrn_out_norm, attn_out_norm, norm_post_mix,
                     norm_pre_ffn, _token_tile(m, 512), 256)
    return ffn(h2, x1, w_gate16, w_up16, w_down16, norm_post_ffn, _token_tile(m, 512), 256)


def kernel(x_prompt, x_sample, cache_k, cache_v, page_table, state_hgrn, norm_pre_mix, w_in,
           hgrn_lb_logits, hgrn_out_norm, attn_out_norm, w_out, norm_post_mix, norm_pre_ffn,
           w_gate, w_up, w_down, norm_post_ffn):
    bp, tp, d = x_prompt.shape
    bs, ts, _ = x_sample.shape
    assert bp == 1
    n_pool, page, h_att, hd = cache_k.shape
    past_len = page_table.shape[1] * page
    slopes = _alibi_slopes(N_ATT_HEADS)
    w_in16, w_out16, w_gate16, w_up16, w_down16 = (
        w.astype(BF16) for w in (w_in, w_out, w_gate, w_up, w_down))
    w16 = (w_out16, w_gate16, w_up16, w_down16)
    norms = (hgrn_out_norm, attn_out_norm, norm_post_mix, norm_pre_ffn, norm_post_ffn)

    xp = x_prompt.reshape(tp, d)
    zp, kp, vp = _project(xp, norm_pre_mix, w_in16)
    s0 = jnp.zeros((bp,) + state_hgrn.shape[1:], state_hgrn.dtype)
    oh_p, state_prompt = hgrn2(zp, hgrn_lb_logits, s0, HEAD_DIM, HEAD_DIM, 4)
    oa_p = moba_prompt(zp, kp, vp, slopes)
    y_prompt = _finish(xp, oh_p, zp, oa_p, w16, norms)

    xs = x_sample.reshape(bs * ts, d)
    zs, ks, vs = _project(xs, norm_pre_mix, w_in16)
    oh_s, state_sample = hgrn2(zs, hgrn_lb_logits, state_hgrn, ts, HGRN_SUB, 4)
    oa_s = moba_sample(zs, ks.reshape(bs * ts * h_att, hd), vs.reshape(bs * ts * h_att, hd),
                       cache_k.reshape(n_pool, page, 2, h_att // 2, hd),
                       cache_v.reshape(n_pool, page, 2, h_att // 2, hd), page_table, slopes, past_len)
    y_sample = _finish(xs, oh_s, zs, oa_s, w16, norms)

    return (y_prompt.reshape(bp, tp, d), y_sample.reshape(bs, ts, d),
            kp.reshape(bp, tp, h_att, hd), vp.reshape(bp, tp, h_att, hd),
            ks.reshape(bs, ts, h_att, hd), vs.reshape(bs, ts, h_att, hd),
            state_prompt, state_sample)
```

```python
import functools

import jax
import jax.numpy as jnp
from jax import lax
from jax.experimental import pallas as pl
from jax.experimental.pallas import tpu as pltpu

F32 = jnp.float32
BF16 = jnp.bfloat16

EPS = 1e-6
HEAD_DIM = 128
N_HGRN_HEADS = 16
N_ATT_HEADS = 16
W_GROUP = N_HGRN_HEADS * HEAD_DIM
MOBA_BLOCK = 256
MOBA_TOPK = 3
PAGE_SIZE = 128
PAGES_PER_BLOCK = MOBA_BLOCK // PAGE_SIZE
ATT_SCALE = HEAD_DIM ** -0.5
HGRN_SUB = 16
NEG_INF = float("-inf")
LOG2E = 1.4426950408889634
MOBA_QB = 2
SAMPLE_BLOCKS_PER_STEP = 2
MOBA_GROUP = 8
VMEM_LIMIT = 56 * 1024 * 1024

ZQ, ZF, ZI, ZG, ZA = 0, 16, 32, 48, 64


def _cparams(sem):
    return pltpu.CompilerParams(dimension_semantics=sem, vmem_limit_bytes=VMEM_LIMIT)


def _nt(a, b, precision=None):
    return lax.dot_general(a, b, (((1,), (1,)), ((), ())), preferred_element_type=F32,
                           precision=precision)


def _sigmoid(x):
    return 1.0 / (1.0 + jnp.exp(-x))


def _norm_kernel(x_ref, g_ref, o_ref):
    x = x_ref[...]
    y = x * lax.rsqrt(jnp.mean(x * x, axis=-1, keepdims=True) + EPS)
    o_ref[...] = (y * g_ref[...]).astype(BF16)


def rmsnorm_bf16(x, g, tm):
    m, d = x.shape
    return pl.pallas_call(
        _norm_kernel,
        out_shape=jax.ShapeDtypeStruct((m, d), BF16),
        grid=(m // tm,),
        in_specs=[pl.BlockSpec((tm, d), lambda i: (i, 0)), pl.BlockSpec((1, d), lambda i: (0, 0))],
        out_specs=pl.BlockSpec((tm, d), lambda i: (i, 0)),
        compiler_params=_cparams(("arbitrary",)),
        name="rmsnorm_bf16",
    )(x, g.reshape(1, d))


def _mm_kernel(a_ref, w_ref, o_ref):
    o_ref[...] = jnp.dot(a_ref[...], w_ref[...].astype(BF16), preferred_element_type=F32)


def matmul_cols(a, w, col0, n, tm, tn):
    m, k = a.shape
    cb = col0 // tn
    return pl.pallas_call(
        _mm_kernel,
        out_shape=jax.ShapeDtypeStruct((m, n), F32),
        grid=(m // tm, n // tn),
        in_specs=[pl.BlockSpec((tm, k), lambda i, j: (i, 0)),
                  pl.BlockSpec((k, tn), lambda i, j: (0, cb + j))],
        out_specs=pl.BlockSpec((tm, tn), lambda i, j: (i, j)),
        compiler_params=_cparams(("arbitrary", "arbitrary")),
        name="matmul_cols",
    )(a, w)


def _cumsum_rows(g):
    c = g.shape[0]
    row = lax.broadcasted_iota(jnp.int32, g.shape, 0)
    b = g
    sh = 1
    while sh < c:
        b = b + jnp.where(row >= sh, pltpu.roll(b, sh, 0), 0.0)
        sh *= 2
    return b


def _hgrn_kernel(q_ref, f_ref, i_ref, lbl_ref, s0_ref, o_ref, sout_ref, st_ref, *, rows_in, rows, hp):
    c = pl.program_id(2)
    nc = pl.num_programs(2)
    for hh in range(hp):
        cols = slice(hh * HEAD_DIM, (hh + 1) * HEAD_DIM)

        @pl.when(c == 0)
        def _(hh=hh):
            st_ref[hh] = s0_ref[0, hh].T

        o, st_new = _hgrn_chunk(q_ref[:, cols], f_ref[:, cols], i_ref[:, cols], lbl_ref[:, cols],
                                st_ref[hh], rows_in, rows)
        o_ref[:, cols] = o
        st_ref[hh] = st_new

        @pl.when(c == nc - 1)
        def _(hh=hh, st_new=st_new):
            sout_ref[0, hh] = st_new.T


def _hgrn_chunk(hq, hf, v, lg, st, rows_in, rows):
    lge = jnp.exp(lg - jnp.max(lg, axis=0, keepdims=True))
    lb = lge[0:1, :] / jnp.sum(lge, axis=0, keepdims=True)

    f = lb + (1.0 - lb) * _sigmoid(hf)
    q = hq * _sigmoid(hq)
    k = 1.0 - f
    g = jnp.log(f)
    if rows > rows_in:
        pad = jnp.zeros((rows - rows_in, HEAD_DIM), F32)
        q, k, g, v = (jnp.concatenate([a, pad], axis=0) for a in (q, k, g, v))
    b = _cumsum_rows(g)

    m = HGRN_SUB
    ns = rows // m

    o = _nt((q * jnp.exp(b)).astype(BF16), st.astype(BF16))

    def exact_diag(width):
        row_w = lax.broadcasted_iota(jnp.int32, (width, HEAD_DIM), 0)
        outs = []
        for i in range(rows // width):
            sl = slice(i * width, (i + 1) * width)
            qi, ki, bi, vi = q[sl], k[sl], b[sl], v[sl]
            oi = jnp.zeros((width, HEAD_DIM), F32)
            for s in range(width):
                e = jnp.exp(jnp.where(row_w >= s, bi - bi[s:s + 1], NEG_INF))
                a = jnp.sum(qi * e * ki[s:s + 1], axis=-1, keepdims=True)
                oi = oi + a * vi[s:s + 1]
            outs.append(oi)
        return outs

    half = m // 2
    if ns >= 2:
        o_diag = exact_diag(half)
        q3, k3, b3, v3 = (a.reshape(ns, m, HEAD_DIM) for a in (q, k, b, v))
        r = b3[:, half - 1:half, :]
        qt = (q3[:, half:, :] * jnp.exp(b3[:, half:, :] - r)).reshape(ns * half, HEAD_DIM)
        kt = (k3[:, :half, :] * jnp.exp(r - b3[:, :half, :])).reshape(ns * half, HEAD_DIM)
        a = _nt(qt.astype(BF16), kt.astype(BF16))
        arow = lax.broadcasted_iota(jnp.int32, a.shape, 0) // half
        acol = lax.broadcasted_iota(jnp.int32, a.shape, 1) // half
        a = jnp.where(arow == acol, a, 0.0)
        low = jnp.dot(a.astype(BF16), v3[:, :half, :].reshape(ns * half, HEAD_DIM).astype(BF16),
                      preferred_element_type=F32)
        for i in range(ns):
            o_diag[2 * i + 1] = o_diag[2 * i + 1] + low[i * half:(i + 1) * half]
    else:
        o_diag = exact_diag(m)
    o = o + jnp.concatenate(o_diag, axis=0) if len(o_diag) > 1 else o + o_diag[0]

    for j in range(ns - 1):
        lo = (j + 1) * m
        r = b[lo - 1:lo]
        kt = k[j * m:lo] * jnp.exp(r - b[j * m:lo])
        qt = q[lo:] * jnp.exp(b[lo:] - r)
        a = _nt(qt.astype(BF16), kt.astype(BF16))
        contrib = jnp.dot(a.astype(BF16), v[j * m:lo].astype(BF16), preferred_element_type=F32)
        o = o + jnp.concatenate([jnp.zeros((lo, HEAD_DIM), F32), contrib], axis=0)

    b_last = b[rows - 1:rows]
    kdec = k * jnp.exp(b_last - b)
    vp, kp = v, kdec
    if rows < HEAD_DIM:
        zp = jnp.zeros((HEAD_DIM - rows, HEAD_DIM), F32)
        vp, kp = jnp.concatenate([v, zp], axis=0), jnp.concatenate([kdec, zp], axis=0)
    st_new = st * jnp.exp(b_last) + jnp.dot(vp.T.astype(BF16), kp.astype(BF16),
                                             preferred_element_type=F32)
    return o[:rows_in], st_new


def hgrn2(z, lb_logits, s0, rows_in, rows, hp):
    bsz = s0.shape[0]
    t = z.shape[0] // bsz
    nc = t // rows_in
    kern = functools.partial(_hgrn_kernel, rows_in=rows_in, rows=rows, hp=hp)
    wide = hp * HEAD_DIM

    def zspec(off):
        return pl.BlockSpec((rows_in, wide), lambda b, h, c: (b * nc + c, off // hp + h))

    sspec = pl.BlockSpec((1, hp, HEAD_DIM, HEAD_DIM), lambda b, h, c: (b, h, 0, 0))
    return pl.pallas_call(
        kern,
        out_shape=(jax.ShapeDtypeStruct((bsz * t, W_GROUP), F32),
                   jax.ShapeDtypeStruct(s0.shape, s0.dtype)),
        grid=(bsz, N_HGRN_HEADS // hp, nc),
        in_specs=[zspec(ZQ), zspec(ZF), zspec(ZI),
                  pl.BlockSpec((lb_logits.shape[0], wide), lambda b, h, c: (0, h)),
                  sspec],
        out_specs=(pl.BlockSpec((rows_in, wide), lambda b, h, c: (b * nc + c, h)), sspec),
        scratch_shapes=[pltpu.VMEM((hp, HEAD_DIM, HEAD_DIM), F32)],
        compiler_params=_cparams(("arbitrary", "arbitrary", "arbitrary")),
        name="hgrn2",
    )(z, z, z, lb_logits, s0)


def _topk_rows_mask(g, topk):
    nb = g.shape[0]
    jj = lax.broadcasted_iota(jnp.int32, g.shape, 0)
    sel = jnp.zeros(g.shape, F32)
    for _ in range(topk):
        mx = jnp.max(g, axis=0, keepdims=True)
        idx = jnp.min(jnp.where(g == mx, jj, nb), axis=0, keepdims=True)
        hit = jj == idx
        sel = jnp.where(jnp.logical_and(hit, mx > NEG_INF), 1.0, sel)
        g = jnp.where(hit, NEG_INF, g)
    return sel


def _split3_bf16(x):
    a = x.astype(BF16)
    r = x - a.astype(F32)
    b = r.astype(BF16)
    return a, b, (r - b.astype(F32)).astype(BF16)


def _moba_prompt_kernel(slope_ref, q_ref, k_ref, v_ref, o_ref,
                        ka_ref, vt_ref, kmean_ref, qa_ref, sel_ref, *, nb, topk, group):
    blk = MOBA_BLOCK
    qw = MOBA_QB * blk
    h = pl.program_id(0)
    qi = pl.program_id(1)
    slope2 = slope_ref[h] * LOG2E

    @pl.when(qi == 0)
    def _():
        lane = lax.broadcasted_iota(jnp.int32, (blk, HEAD_DIM), 1)
        rloc = lax.broadcasted_iota(jnp.int32, (blk, HEAD_DIM), 0).astype(F32)
        b1, b2, b3 = _split3_bf16(slope2 * rloc)
        aug = jnp.where(lane == 0, b1.astype(F32),
                        jnp.where(lane == 1, b2.astype(F32),
                                  jnp.where(lane == 2, b3.astype(F32), 0.0))).astype(BF16)

        def prep(j, carry):
            r0 = pl.multiple_of(j * blk, blk)
            kj = k_ref[pl.ds(r0, blk), :]
            ka_ref[pl.ds(r0, blk), :] = jnp.concatenate([kj.astype(BF16), aug], axis=1)
            vt_ref[:, pl.ds(r0, blk)] = v_ref[pl.ds(r0, blk), :].T.astype(BF16)
            rowj = lax.broadcasted_iota(jnp.int32, (nb, HEAD_DIM), 0)
            kmean_ref[...] = jnp.where(rowj == j, jnp.mean(kj, axis=0, keepdims=True), kmean_ref[...])
            return carry
        lax.fori_loop(0, nb, prep, 0)

    q = q_ref[...] * ATT_SCALE
    lane_q = lax.broadcasted_iota(jnp.int32, (qw, HEAD_DIM), 1)
    ones = jnp.where(lane_q < 3, 1.0, 0.0).astype(BF16)
    qa_ref[...] = jnp.concatenate([(q * LOG2E).astype(BF16), ones], axis=1)

    gate = _nt(kmean_ref[...], q, precision=lax.Precision.HIGHEST)
    jj = lax.broadcasted_iota(jnp.int32, (nb, qw), 0)
    qb_lane = MOBA_QB * qi + lax.broadcasted_iota(jnp.int32, (nb, qw), 1) // blk
    sel_ref[...] = _topk_rows_mask(jnp.where(jj < qb_lane, gate, NEG_INF), topk)

    r0 = pl.multiple_of(qi * qw, qw)
    tt = lax.broadcasted_iota(jnp.int32, (qw, qw), 1)
    rr = lax.broadcasted_iota(jnp.int32, (qw, qw), 0)
    own = jnp.logical_and(tt >= rr, tt // blk == rr // blk)
    s = jnp.where(own, _nt(ka_ref[pl.ds(r0, qw), :], qa_ref[...]), NEG_INF)
    cq = slope2 * (qb_lane[0:1, :] * blk).astype(F32)
    smax = jnp.max(s, axis=0, keepdims=True)
    p = jnp.exp2(s - smax)
    m0 = smax + cq
    l0 = jnp.sum(p, axis=0, keepdims=True)
    acc0 = jnp.dot(vt_ref[:, pl.ds(r0, qw)], p.astype(BF16), preferred_element_type=F32)

    def body(g, carry):
        m, l, acc = carry
        j0 = g * group
        c0 = pl.multiple_of(j0 * blk, group * blk)
        selg = sel_ref[pl.ds(pl.multiple_of(j0, group), group), :]
        s = _nt(ka_ref[pl.ds(c0, group * blk), :], qa_ref[...])
        ons, cjs = [], []
        mn = m
        for b in range(group):
            cj = slope2 * ((j0 + b) * blk).astype(F32)
            on = selg[b:b + 1, :] > 0.0
            cm = jnp.max(s[b * blk:(b + 1) * blk], axis=0, keepdims=True) + cj
            mn = jnp.maximum(mn, jnp.where(on, cm, NEG_INF))
            ons.append(on)
            cjs.append(cj)
        ps = [jnp.exp2(s[b * blk:(b + 1) * blk] - jnp.where(ons[b], mn - cjs[b], float("inf")))
              for b in range(group)]
        alpha = jnp.exp2(m - mn)
        lsum = ps[0].sum(axis=0, keepdims=True)
        for b in range(1, group):
            lsum = lsum + ps[b].sum(axis=0, keepdims=True)
        p = jnp.concatenate([x.astype(BF16) for x in ps], axis=0)
        pv = jnp.dot(vt_ref[:, pl.ds(c0, group * blk)], p, preferred_element_type=F32)
        return mn, l * alpha + lsum, acc * alpha + pv

    n_groups = (MOBA_QB * qi + MOBA_QB - 1 + group - 1) // group
    _, l, acc = lax.fori_loop(0, n_groups, body, (m0, l0, acc0))
    o_ref[...] = (acc / l).T


def moba_prompt(z, k, v, slopes):
    t = k.shape[0]
    nb = t // MOBA_BLOCK
    topk = min(MOBA_TOPK, nb - 1)
    group = next(g for g in (MOBA_GROUP, 4, 2, 1) if nb % g == 0)
    qw = MOBA_QB * MOBA_BLOCK
    kern = functools.partial(_moba_prompt_kernel, nb=nb, topk=topk, group=group)
    grid_spec = pltpu.PrefetchScalarGridSpec(
        num_scalar_prefetch=1,
        grid=(N_ATT_HEADS, nb // MOBA_QB),
        in_specs=[pl.BlockSpec((qw, HEAD_DIM), lambda h, qi, sl: (qi, ZA + h)),
                  pl.BlockSpec((t, HEAD_DIM), lambda h, qi, sl: (0, h)),
                  pl.BlockSpec((t, HEAD_DIM), lambda h, qi, sl: (0, h))],
        out_specs=pl.BlockSpec((qw, HEAD_DIM), lambda h, qi, sl: (qi, h)),
        scratch_shapes=[pltpu.VMEM((t, 2 * HEAD_DIM), BF16),
                        pltpu.VMEM((HEAD_DIM, t), BF16),
                        pltpu.VMEM((nb, HEAD_DIM), F32),
                        pltpu.VMEM((qw, 2 * HEAD_DIM), BF16),
                        pltpu.VMEM((nb, qw), F32)],
    )
    return pl.pallas_call(
        kern,
        out_shape=jax.ShapeDtypeStruct((t, N_ATT_HEADS * HEAD_DIM), F32),
        grid_spec=grid_spec,
        compiler_params=_cparams(("arbitrary", "arbitrary")),
        name="moba_prompt",
    )(slopes, z, k, v)


def _lane_topk_mask(g, topk, n_valid):
    lane = lax.broadcasted_iota(jnp.int32, g.shape, 1).astype(F32)
    g = jnp.where(lane < n_valid, g, NEG_INF)
    sel = jnp.zeros(g.shape, F32)
    for _ in range(topk):
        mx = jnp.max(g, axis=1, keepdims=True)
        idx = jnp.min(jnp.where(g == mx, lane, float(g.shape[1])), axis=1, keepdims=True)
        hit = lane == idx
        sel = jnp.where(jnp.logical_and(hit, mx > NEG_INF), 1.0, sel)
        g = jnp.where(hit, NEG_INF, g)
    return sel


def _moba_sample_kernel(pt_ref, slope_ref, q_ref, kn_ref, vn_ref, *refs, nb, nt, past_len, bps):
    del pt_ref
    npg = bps * PAGES_PER_BLOCK
    ck_refs, cv_refs = refs[:npg], refs[npg:2 * npg]
    o_ref, qf_ref, q16_ref, bias_ref, ks_ref, m_ref, l_ref, acc_ref = refs[2 * npg:]
    blk = MOBA_BLOCK
    nh = N_ATT_HEADS
    hh = nh // 2
    nl = nh * nt
    nlh = hh * nt
    wide = blk * hh
    step = pl.program_id(1)
    slope2 = slope_ref[...] * LOG2E

    @pl.when(step == 0)
    def _():
        aq = q_ref[...] * ATT_SCALE
        qf = jnp.concatenate([aq[:, h * HEAD_DIM:(h + 1) * HEAD_DIM] for h in range(nh)], axis=0)
        qf_ref[...] = qf
        q16_ref[...] = (qf * LOG2E).astype(BF16)
        qrow = lax.broadcasted_iota(jnp.int32, (nl, wide), 0)
        kcol = lax.broadcasted_iota(jnp.int32, (nl, wide), 1)
        same_head = kcol % hh == (qrow // nt) % hh
        bias_ref[...] = jnp.where(same_head, -slope2 * (qrow % nt - kcol // hh).astype(F32), NEG_INF)
        m_ref[...] = jnp.zeros(m_ref.shape, F32)
        l_ref[...] = jnp.zeros(l_ref.shape, F32)

    lane = lax.broadcasted_iota(jnp.int32, (nl, HEAD_DIM), 1)
    m_all, l_all = m_ref[...], l_ref[...]
    for bb in range(bps):
        j = step * bps + bb
        pages = [PAGES_PER_BLOCK * bb + i for i in range(PAGES_PER_BLOCK)]
        mjs, ljs, accs = [], [], []
        for g in range(2):
            rows = slice(g * nlh, (g + 1) * nlh)
            kg = jnp.concatenate([ck_refs[i][0, :, g].reshape(PAGE_SIZE * hh, HEAD_DIM)
                                  for i in pages], axis=0)
            vg = jnp.concatenate([cv_refs[i][0, :, g].reshape(PAGE_SIZE * hh, HEAD_DIM)
                                  for i in pages], axis=0)
            ks_ref[pl.ds(pl.multiple_of(j * nh + g * hh, hh), hh), :] = jnp.sum(
                kg.reshape(blk, hh, HEAD_DIM), axis=0)
            s = _nt(q16_ref[rows, :], kg.astype(BF16)) + bias_ref[rows, :]
            mj = jnp.max(s, axis=1, keepdims=True)
            p = jnp.exp2(s - mj)
            mjs.append(mj)
            ljs.append(jnp.sum(p, axis=1, keepdims=True))
            accs.append(jnp.dot(p.astype(BF16), vg.astype(BF16), preferred_element_type=F32))
        acc_ref[j] = jnp.concatenate(accs, axis=0)
        off = (past_len - j * blk).astype(F32)
        m_all = jnp.where(lane == j, jnp.concatenate(mjs, axis=0) - slope2 * off, m_all)
        l_all = jnp.where(lane == j, jnp.concatenate(ljs, axis=0), l_all)
    m_ref[...] = m_all
    l_ref[...] = l_all

    @pl.when(step == nb // bps - 1)
    def _():
        qf = qf_ref[...]
        qrow = lax.broadcasted_iota(jnp.int32, (nl, nb * nh), 0)
        gcol = lax.broadcasted_iota(jnp.int32, (nl, nb * nh), 1)
        g_all = _nt(qf, ks_ref[...] * (1.0 / blk), precision=lax.Precision.HIGHEST)
        g_own = jnp.where(gcol % nh == qrow // nt, g_all, 0.0)
        erow = lax.broadcasted_iota(jnp.int32, (nb * nh, HEAD_DIM), 0)
        ecol = lax.broadcasted_iota(jnp.int32, (nb * nh, HEAD_DIM), 1)
        pick = jnp.where(erow // nh == ecol, 1.0, 0.0)
        gate = jnp.dot(g_own, pick, preferred_element_type=F32, precision=lax.Precision.HIGHEST)
        w = _lane_topk_mask(gate, min(MOBA_TOPK, nb), nb)

        orow = lax.broadcasted_iota(jnp.int32, (nl, nl), 0)
        ocol = lax.broadcasted_iota(jnp.int32, (nl, nl), 1)
        dist = orow % nt - ocol // nh
        ok = jnp.logical_and(ocol % nh == orow // nt, dist >= 0)
        so = jnp.where(ok, _nt(q16_ref[...], kn_ref[...].astype(BF16)) - slope2 * dist.astype(F32),
                       NEG_INF)
        mo = jnp.max(so, axis=1, keepdims=True)

        mall = m_ref[...]
        mtot = jnp.maximum(mo, jnp.max(jnp.where(w > 0.0, mall, NEG_INF), axis=1, keepdims=True))
        wj = jnp.where(w > 0.0, jnp.exp2(mall - mtot), 0.0)
        po = jnp.exp2(so - mtot)
        denom = jnp.sum(wj * l_ref[...], axis=1, keepdims=True) + jnp.sum(po, axis=1, keepdims=True)
        num = jnp.dot(po.astype(BF16), vn_ref[...].astype(BF16), preferred_element_type=F32)
        for jb in range(nb):
            num = num + wj[:, jb:jb + 1] * acc_ref[jb]
        out = num / denom
        for h in range(nh):
            o_ref[:, h * HEAD_DIM:(h + 1) * HEAD_DIM] = out[h * nt:(h + 1) * nt, :]


def moba_sample(z, k_new, v_new, cache_k, cache_v, page_table, slopes, past_len):
    bsz, n_pages = page_table.shape
    nh = N_ATT_HEADS
    nt = k_new.shape[0] // (bsz * nh)
    nb = past_len // MOBA_BLOCK
    assert n_pages == nb * PAGES_PER_BLOCK and PAGES_PER_BLOCK == 2
    assert nh * nt == HEAD_DIM and nb <= HEAD_DIM
    wd = nh * HEAD_DIM
    slope_col = jnp.repeat(slopes, nt).reshape(nh * nt, 1)
    bps = SAMPLE_BLOCKS_PER_STEP
    assert nb % bps == 0
    npg = bps * PAGES_PER_BLOCK
    kern = functools.partial(_moba_sample_kernel, nb=nb, nt=nt, past_len=past_len, bps=bps)

    def page_spec(which):
        return pl.BlockSpec((1, PAGE_SIZE, 2, nh // 2, HEAD_DIM),
                            lambda b, j, pt: (pt[b * n_pages + npg * j + which], 0, 0, 0, 0))

    new_spec = pl.BlockSpec((nt * nh, HEAD_DIM), lambda b, j, pt: (b, 0))
    grid_spec = pltpu.PrefetchScalarGridSpec(
        num_scalar_prefetch=1,
        grid=(bsz, nb // bps),
        in_specs=[pl.BlockSpec((nh * nt, 1), lambda b, j, pt: (0, 0)),
                  pl.BlockSpec((nt, wd), lambda b, j, pt: (b, ZA * HEAD_DIM // wd)),
                  new_spec, new_spec] + [page_spec(i) for i in range(npg)] * 2,
        out_specs=pl.BlockSpec((nt, wd), lambda b, j, pt: (b, 0)),
        scratch_shapes=[pltpu.VMEM((nh * nt, HEAD_DIM), F32),
                        pltpu.VMEM((nh * nt, HEAD_DIM), BF16),
                        pltpu.VMEM((nh * nt, MOBA_BLOCK * nh // 2), F32),
                        pltpu.VMEM((nb * nh, HEAD_DIM), F32),
                        pltpu.VMEM((nh * nt, HEAD_DIM), F32),
                        pltpu.VMEM((nh * nt, HEAD_DIM), F32),
                        pltpu.VMEM((nb, nh * nt, HEAD_DIM), F32)],
    )
    return pl.pallas_call(
        kern,
        out_shape=jax.ShapeDtypeStruct((bsz * nt, wd), F32),
        grid_spec=grid_spec,
        compiler_params=_cparams(("arbitrary", "arbitrary")),
        name="moba_sample",
    )(page_table.reshape(-1), slope_col, z, k_new, v_new, *([cache_k] * npg + [cache_v] * npg))


def _mix_lhs_kernel(oh_ref, g_ref, oa_ref, hn_ref, an_ref, o_ref):
    hn = hn_ref[...]
    for h in range(N_HGRN_HEADS):
        sl = slice(h * HEAD_DIM, (h + 1) * HEAD_DIM)
        o = oh_ref[:, sl]
        y = o * lax.rsqrt(jnp.mean(o * o, axis=-1, keepdims=True) + EPS) * hn
        gh = g_ref[:, sl]
        o_ref[:, sl] = (y * (gh * _sigmoid(gh))).astype(BF16)
    oa = oa_ref[...]
    ya = oa * lax.rsqrt(jnp.mean(oa * oa, axis=-1, keepdims=True) + EPS) * an_ref[...]
    o_ref[:, W_GROUP:] = ya.astype(BF16)


def mix_lhs(o_hgrn, z, o_att, hgrn_out_norm, attn_out_norm, tm):
    m = o_hgrn.shape[0]
    gsec = ZG * HEAD_DIM // W_GROUP
    return pl.pallas_call(
        _mix_lhs_kernel,
        out_shape=jax.ShapeDtypeStruct((m, 2 * W_GROUP), BF16),
        grid=(m // tm,),
        in_specs=[pl.BlockSpec((tm, W_GROUP), lambda i: (i, 0)),
                  pl.BlockSpec((tm, W_GROUP), lambda i: (i, gsec)),
                  pl.BlockSpec((tm, W_GROUP), lambda i: (i, 0)),
                  pl.BlockSpec((1, HEAD_DIM), lambda i: (0, 0)),
                  pl.BlockSpec((1, W_GROUP), lambda i: (0, 0))],
        out_specs=pl.BlockSpec((tm, 2 * W_GROUP), lambda i: (i, 0)),
        compiler_params=_cparams(("arbitrary",)),
        name="mix_lhs",
    )(o_hgrn, z, o_att, hgrn_out_norm.reshape(1, -1), attn_out_norm.reshape(1, -1))


def _mix_out_kernel(c_ref, x_ref, w_ref, pm_ref, pf_ref, x1_ref, h2_ref, *, tn):
    j = pl.program_id(1)
    nj = pl.num_programs(1)
    c0 = pl.multiple_of(j * tn, tn)
    x1_ref[:, pl.ds(c0, tn)] = jnp.dot(c_ref[...], w_ref[...], preferred_element_type=F32)

    @pl.when(j == nj - 1)
    def _():
        mix = x1_ref[...]
        x1 = x_ref[...] + mix * lax.rsqrt(jnp.mean(mix * mix, axis=-1, keepdims=True) + EPS) * pm_ref[...]
        x1_ref[...] = x1
        h2 = x1 * lax.rsqrt(jnp.mean(x1 * x1, axis=-1, keepdims=True) + EPS) * pf_ref[...]
        h2_ref[...] = h2.astype(BF16)


def mix_out(lhs, x, w_out, norm_post_mix, norm_pre_ffn, tm, tn):
    m, d = x.shape
    kern = functools.partial(_mix_out_kernel, tn=tn)
    vec = lambda n: pl.BlockSpec((1, n), lambda i, j: (0, 0))
    return pl.pallas_call(
        kern,
        out_shape=(jax.ShapeDtypeStruct((m, d), F32), jax.ShapeDtypeStruct((m, d), BF16)),
        grid=(m // tm, d // tn),
        in_specs=[pl.BlockSpec((tm, lhs.shape[1]), lambda i, j: (i, 0)),
                  pl.BlockSpec((tm, d), lambda i, j: (i, 0), pipeline_mode=pl.Buffered(1)),
                  pl.BlockSpec((lhs.shape[1], tn), lambda i, j: (0, j)),
                  vec(d), vec(d)],
        out_specs=(pl.BlockSpec((tm, d), lambda i, j: (i, 0)),
                   pl.BlockSpec((tm, d), lambda i, j: (i, 0))),
        compiler_params=_cparams(("arbitrary", "arbitrary")),
        name="mix_out",
    )(lhs, x, w_out, norm_post_mix.reshape(1, -1), norm_pre_ffn.reshape(1, -1))


def _ffn_kernel(h_ref, wg_ref, wu_ref, wd_ref, x1_ref, pn_ref, y_ref):
    f = pl.program_id(1)
    nf = pl.num_programs(1)

    @pl.when(f == 0)
    def _():
        y_ref[...] = jnp.zeros(y_ref.shape, F32)

    h = h_ref[...]
    g = jnp.dot(h, wg_ref[...], preferred_element_type=F32)
    u = jnp.dot(h, wu_ref[...], preferred_element_type=F32)
    a = (g * _sigmoid(g) * u).astype(BF16)
    y_ref[...] += jnp.dot(a, wd_ref[...], preferred_element_type=F32)

    @pl.when(f == nf - 1)
    def _():
        ff = y_ref[...]
        y_ref[...] = x1_ref[...] + ff * lax.rsqrt(jnp.mean(ff * ff, axis=-1, keepdims=True) + EPS) * pn_ref[...]


def ffn(h2, x1, w_gate, w_up, w_down, norm_post_ffn, tm, tf):
    m, d = h2.shape
    dff = w_gate.shape[1]
    once = pl.Buffered(1)
    return pl.pallas_call(
        _ffn_kernel,
        out_shape=jax.ShapeDtypeStruct((m, d), F32),
        grid=(m // tm, dff // tf),
        in_specs=[pl.BlockSpec((tm, d), lambda i, f: (i, 0), pipeline_mode=once),
                  pl.BlockSpec((d, tf), lambda i, f: (0, f)),
                  pl.BlockSpec((d, tf), lambda i, f: (0, f)),
                  pl.BlockSpec((tf, d), lambda i, f: (f, 0)),
                  pl.BlockSpec((tm, d), lambda i, f: (i, 0), pipeline_mode=once),
                  pl.BlockSpec((1, d), lambda i, f: (0, 0))],
        out_specs=pl.BlockSpec((tm, d), lambda i, f: (i, 0)),
        compiler_params=_cparams(("arbitrary", "arbitrary")),
        name="ffn",
    )(h2, w_gate, w_up, w_down, x1, norm_post_ffn.reshape(1, -1))


def _alibi_slopes(n):
    return 2.0 ** (-8.0 * jnp.arange(1, n + 1, dtype=F32) / n)


def _token_tile(m, cap):
    return cap if m % cap == 0 else m


def _project(x2d, norm_pre_mix, w_in):
    m = x2d.shape[0]
    h = rmsnorm_bf16(x2d, norm_pre_mix, _token_tile(m, 256))
    tm = _token_tile(m, 1024)
    z = matmul_cols(h, w_in, 0, 5 * W_GROUP, tm, 512)
    k = matmul_cols(h, w_in, 5 * W_GROUP, W_GROUP, tm, 512)
    v = matmul_cols(h, w_in, 6 * W_GROUP, W_GROUP, tm, 512)
    return z, k, v


def _finish(x2d, o_hgrn, z, o_att, w16, norms):
    hgrn_out_norm, attn_out_norm, norm_post_mix, norm_pre_ffn, norm_post_ffn = norms
    w_out16, w_gate16, w_up16, w_down16 = w16
    m = x2d.shape[0]
    lhs = mix_lhs(o_hgrn, z, o_att, hgrn_out_norm, attn_out_norm, _token_tile(m, 256))
    x1, h2 = mix_out(lhs, x2d, w_out16, norm_post_mix, norm_pre_ffn, _token_tile(m, 512), 512)
    return ffn(h2, x1, w_gate16, w_up16, w_down16, norm_post_ffn, _token_tile(m, 512), 256)


def kernel(x_prompt, x_sample, cache_k, cache_v, page_table, state_hgrn, norm_pre_mix, w_in,
           hgrn_lb_logits, hgrn_out_norm, attn_out_norm, w_out, norm_post_mix, norm_pre_ffn,
           w_gate, w_up, w_down, norm_post_ffn):
    bp, tp, d = x_prompt.shape
    bs, ts, _ = x_sample.shape
    assert bp == 1
    n_pool, page, h_att, hd = cache_k.shape
    past_len = page_table.shape[1] * page
    slopes = _alibi_slopes(N_ATT_HEADS)
    w_out16, w_gate16, w_up16, w_down16 = (w.astype(BF16) for w in (w_out, w_gate, w_up, w_down))
    w16 = (w_out16, w_gate16, w_up16, w_down16)
    norms = (hgrn_out_norm, attn_out_norm, norm_post_mix, norm_pre_ffn, norm_post_ffn)

    xp = x_prompt.reshape(tp, d)
    zp, kp, vp = _project(xp, norm_pre_mix, w_in)
    s0 = jnp.zeros((bp,) + state_hgrn.shape[1:], state_hgrn.dtype)
    oh_p, state_prompt = hgrn2(zp, hgrn_lb_logits, s0, HEAD_DIM, HEAD_DIM, 4)
    oa_p = moba_prompt(zp, kp, vp, slopes)
    y_prompt = _finish(xp, oh_p, zp, oa_p, w16, norms)

    xs = x_sample.reshape(bs * ts, d)
    zs, ks, vs = _project(xs, norm_pre_mix, w_in)
    oh_s, state_sample = hgrn2(zs, hgrn_lb_logits, state_hgrn, ts, HGRN_SUB, 4)
    oa_s = moba_sample(zs, ks.reshape(bs * ts * h_att, hd), vs.reshape(bs * ts * h_att, hd),
                       cache_k.reshape(n_pool, page, 2, h_att // 2, hd),
                       cache_v.reshape(n_pool, page, 2, h_att // 2, hd), page_table, slopes, past_len)
    y_sample = _finish(xs, oh_s, zs, oa_s, w16, norms)

    return (y_prompt.reshape(bp, tp, d), y_sample.reshape(bs, ts, d),
            kp.reshape(bp, tp, h_att, hd), vp.reshape(bp, tp, h_att, hd),
            ks.reshape(bs, ts, h_att, hd), vs.reshape(bs, ts, h_att, hd),
            state_prompt, state_sample)
```

```python
import functools

import jax
import jax.numpy as jnp
from jax import lax
from jax.experimental import pallas as pl
from jax.experimental.pallas import tpu as pltpu

F32 = jnp.float32
BF16 = jnp.bfloat16

EPS = 1e-6
HEAD_DIM = 128
N_HGRN_HEADS = 16
N_ATT_HEADS = 16
W_GROUP = N_HGRN_HEADS * HEAD_DIM
MOBA_BLOCK = 256
MOBA_TOPK = 3
PAGE_SIZE = 128
PAGES_PER_BLOCK = MOBA_BLOCK // PAGE_SIZE
ATT_SCALE = HEAD_DIM ** -0.5
HGRN_SUB = 16
NEG_INF = float("-inf")
LOG2E = 1.4426950408889634
MOBA_QB = 4
SAMPLE_BLOCKS_PER_STEP = 4
MOBA_GROUP = 8
VMEM_LIMIT = 56 * 1024 * 1024

ZQ, ZF, ZI, ZG, ZA = 0, 16, 32, 48, 64


def _cparams(sem):
    return pltpu.CompilerParams(dimension_semantics=sem, vmem_limit_bytes=VMEM_LIMIT)


def _nt(a, b, precision=None):
    return lax.dot_general(a, b, (((1,), (1,)), ((), ())), preferred_element_type=F32,
                           precision=precision)


def _sigmoid(x):
    return 1.0 / (1.0 + jnp.exp(-x))


def _norm_kernel(x_ref, g_ref, o_ref):
    x = x_ref[...]
    y = x * lax.rsqrt(jnp.mean(x * x, axis=-1, keepdims=True) + EPS)
    o_ref[...] = (y * g_ref[...]).astype(BF16)


def rmsnorm_bf16(x, g, tm):
    m, d = x.shape
    return pl.pallas_call(
        _norm_kernel,
        out_shape=jax.ShapeDtypeStruct((m, d), BF16),
        grid=(m // tm,),
        in_specs=[pl.BlockSpec((tm, d), lambda i: (i, 0)), pl.BlockSpec((1, d), lambda i: (0, 0))],
        out_specs=pl.BlockSpec((tm, d), lambda i: (i, 0)),
        compiler_params=_cparams(("arbitrary",)),
        name="rmsnorm_bf16",
    )(x, g.reshape(1, d))


def _mm_kernel(a_ref, w_ref, o_ref):
    o_ref[...] = jnp.dot(a_ref[...], w_ref[...], preferred_element_type=F32)


def matmul_cols(a, w, col0, n, tm, tn):
    m, k = a.shape
    cb = col0 // tn
    return pl.pallas_call(
        _mm_kernel,
        out_shape=jax.ShapeDtypeStruct((m, n), F32),
        grid=(m // tm, n // tn),
        in_specs=[pl.BlockSpec((tm, k), lambda i, j: (i, 0)),
                  pl.BlockSpec((k, tn), lambda i, j: (0, cb + j))],
        out_specs=pl.BlockSpec((tm, tn), lambda i, j: (i, j)),
        compiler_params=_cparams(("arbitrary", "arbitrary")),
        name="matmul_cols",
    )(a, w)


def _cumsum_rows(g):
    c = g.shape[0]
    row = lax.broadcasted_iota(jnp.int32, g.shape, 0)
    b = g
    sh = 1
    while sh < c:
        b = b + jnp.where(row >= sh, pltpu.roll(b, sh, 0), 0.0)
        sh *= 2
    return b


def _hgrn_kernel(q_ref, f_ref, i_ref, lbl_ref, s0_ref, o_ref, sout_ref, st_ref, *, rows_in, rows, hp):
    c = pl.program_id(2)
    nc = pl.num_programs(2)
    for hh in range(hp):
        cols = slice(hh * HEAD_DIM, (hh + 1) * HEAD_DIM)

        @pl.when(c == 0)
        def _(hh=hh):
            st_ref[hh] = s0_ref[0, hh].T

        o, st_new = _hgrn_chunk(q_ref[:, cols], f_ref[:, cols], i_ref[:, cols], lbl_ref[:, cols],
                                st_ref[hh], rows_in, rows)
        o_ref[:, cols] = o
        st_ref[hh] = st_new

        @pl.when(c == nc - 1)
        def _(hh=hh, st_new=st_new):
            sout_ref[0, hh] = st_new.T


def _hgrn_chunk(hq, hf, v, lg, st, rows_in, rows):
    lge = jnp.exp(lg - jnp.max(lg, axis=0, keepdims=True))
    lb = lge[0:1, :] / jnp.sum(lge, axis=0, keepdims=True)

    f = lb + (1.0 - lb) * _sigmoid(hf)
    q = hq * _sigmoid(hq)
    k = 1.0 - f
    g = jnp.log(f)
    if rows > rows_in:
        pad = jnp.zeros((rows - rows_in, HEAD_DIM), F32)
        q, k, g, v = (jnp.concatenate([a, pad], axis=0) for a in (q, k, g, v))
    b = _cumsum_rows(g)

    m = HGRN_SUB
    ns = rows // m

    o = _nt((q * jnp.exp(b)).astype(BF16), st.astype(BF16))

    def exact_diag(width):
        row_w = lax.broadcasted_iota(jnp.int32, (width, HEAD_DIM), 0)
        outs = []
        for i in range(rows // width):
            sl = slice(i * width, (i + 1) * width)
            qi, ki, bi, vi = q[sl], k[sl], b[sl], v[sl]
            oi = jnp.zeros((width, HEAD_DIM), F32)
            for s in range(width):
                e = jnp.exp(jnp.where(row_w >= s, bi - bi[s:s + 1], NEG_INF))
                a = jnp.sum(qi * e * ki[s:s + 1], axis=-1, keepdims=True)
                oi = oi + a * vi[s:s + 1]
            outs.append(oi)
        return outs

    half = m // 2
    if ns >= 2:
        o_diag = exact_diag(half)
        q3, k3, b3, v3 = (a.reshape(ns, m, HEAD_DIM) for a in (q, k, b, v))
        r = b3[:, half - 1:half, :]
        qt = (q3[:, half:, :] * jnp.exp(b3[:, half:, :] - r)).reshape(ns * half, HEAD_DIM)
        kt = (k3[:, :half, :] * jnp.exp(r - b3[:, :half, :])).reshape(ns * half, HEAD_DIM)
        a = _nt(qt.astype(BF16), kt.astype(BF16))
        arow = lax.broadcasted_iota(jnp.int32, a.shape, 0) // half
        acol = lax.broadcasted_iota(jnp.int32, a.shape, 1) // half
        a = jnp.where(arow == acol, a, 0.0)
        low = jnp.dot(a.astype(BF16), v3[:, :half, :].reshape(ns * half, HEAD_DIM).astype(BF16),
                      preferred_element_type=F32)
        for i in range(ns):
            o_diag[2 * i + 1] = o_diag[2 * i + 1] + low[i * half:(i + 1) * half]
    else:
        o_diag = exact_diag(m)
    o = o + jnp.concatenate(o_diag, axis=0) if len(o_diag) > 1 else o + o_diag[0]

    for j in range(ns - 1):
        lo = (j + 1) * m
        r = b[lo - 1:lo]
        kt = k[j * m:lo] * jnp.exp(r - b[j * m:lo])
        qt = q[lo:] * jnp.exp(b[lo:] - r)
        a = _nt(qt.astype(BF16), kt.astype(BF16))
        contrib = jnp.dot(a.astype(BF16), v[j * m:lo].astype(BF16), preferred_element_type=F32)
        o = o + jnp.concatenate([jnp.zeros((lo, HEAD_DIM), F32), contrib], axis=0)

    b_last = b[rows - 1:rows]
    kdec = k * jnp.exp(b_last - b)
    vp, kp = v, kdec
    if rows < HEAD_DIM:
        zp = jnp.zeros((HEAD_DIM - rows, HEAD_DIM), F32)
        vp, kp = jnp.concatenate([v, zp], axis=0), jnp.concatenate([kdec, zp], axis=0)
    st_new = st * jnp.exp(b_last) + jnp.dot(vp.T.astype(BF16), kp.astype(BF16),
                                             preferred_element_type=F32)
    return o[:rows_in], st_new


def hgrn2(z, lb_logits, s0, rows_in, rows, hp):
    bsz = s0.shape[0]
    t = z.shape[0] // bsz
    nc = t // rows_in
    kern = functools.partial(_hgrn_kernel, rows_in=rows_in, rows=rows, hp=hp)
    wide = hp * HEAD_DIM

    def zspec(off):
        return pl.BlockSpec((rows_in, wide), lambda b, h, c: (b * nc + c, off // hp + h))

    sspec = pl.BlockSpec((1, hp, HEAD_DIM, HEAD_DIM), lambda b, h, c: (b, h, 0, 0))
    return pl.pallas_call(
        kern,
        out_shape=(jax.ShapeDtypeStruct((bsz * t, W_GROUP), F32),
                   jax.ShapeDtypeStruct(s0.shape, s0.dtype)),
        grid=(bsz, N_HGRN_HEADS // hp, nc),
        in_specs=[zspec(ZQ), zspec(ZF), zspec(ZI),
                  pl.BlockSpec((lb_logits.shape[0], wide), lambda b, h, c: (0, h)),
                  sspec],
        out_specs=(pl.BlockSpec((rows_in, wide), lambda b, h, c: (b * nc + c, h)), sspec),
        scratch_shapes=[pltpu.VMEM((hp, HEAD_DIM, HEAD_DIM), F32)],
        compiler_params=_cparams(("arbitrary", "arbitrary", "arbitrary")),
        name="hgrn2",
    )(z, z, z, lb_logits, s0)


def _topk_rows_mask(g, topk):
    nb = g.shape[0]
    jj = lax.broadcasted_iota(jnp.int32, g.shape, 0)
    sel = jnp.zeros(g.shape, F32)
    for _ in range(topk):
        mx = jnp.max(g, axis=0, keepdims=True)
        idx = jnp.min(jnp.where(g == mx, jj, nb), axis=0, keepdims=True)
        hit = jj == idx
        sel = jnp.where(jnp.logical_and(hit, mx > NEG_INF), 1.0, sel)
        g = jnp.where(hit, NEG_INF, g)
    return sel


def _split3_bf16(x):
    a = x.astype(BF16)
    r = x - a.astype(F32)
    b = r.astype(BF16)
    return a, b, (r - b.astype(F32)).astype(BF16)


def _moba_prompt_kernel(slope_ref, q_ref, k_ref, v_ref, o_ref,
                        ka_ref, vt_ref, kmean_ref, qa_ref, sel_ref, *, nb, topk, group):
    blk = MOBA_BLOCK
    qw = MOBA_QB * blk
    h = pl.program_id(0)
    qi = pl.program_id(1)
    slope2 = slope_ref[h] * LOG2E

    @pl.when(qi == 0)
    def _():
        lane = lax.broadcasted_iota(jnp.int32, (blk, HEAD_DIM), 1)
        rloc = lax.broadcasted_iota(jnp.int32, (blk, HEAD_DIM), 0).astype(F32)
        b1, b2, b3 = _split3_bf16(slope2 * rloc)
        aug = jnp.where(lane == 0, b1.astype(F32),
                        jnp.where(lane == 1, b2.astype(F32),
                                  jnp.where(lane == 2, b3.astype(F32), 0.0))).astype(BF16)

        def prep(j, carry):
            r0 = pl.multiple_of(j * blk, blk)
            kj = k_ref[pl.ds(r0, blk), :]
            ka_ref[pl.ds(r0, blk), :] = jnp.concatenate([kj.astype(BF16), aug], axis=1)
            vt_ref[:, pl.ds(r0, blk)] = v_ref[pl.ds(r0, blk), :].T.astype(BF16)
            rowj = lax.broadcasted_iota(jnp.int32, (nb, HEAD_DIM), 0)
            kmean_ref[...] = jnp.where(rowj == j, jnp.mean(kj, axis=0, keepdims=True), kmean_ref[...])
            return carry
        lax.fori_loop(0, nb, prep, 0)

    q = q_ref[...] * ATT_SCALE
    lane_q = lax.broadcasted_iota(jnp.int32, (qw, HEAD_DIM), 1)
    ones = jnp.where(lane_q < 3, 1.0, 0.0).astype(BF16)
    qa_ref[...] = jnp.concatenate([(q * LOG2E).astype(BF16), ones], axis=1)

    gate = _nt(kmean_ref[...], q, precision=lax.Precision.HIGHEST)
    jj = lax.broadcasted_iota(jnp.int32, (nb, qw), 0)
    qb_lane = MOBA_QB * qi + lax.broadcasted_iota(jnp.int32, (nb, qw), 1) // blk
    sel_ref[...] = _topk_rows_mask(jnp.where(jj < qb_lane, gate, NEG_INF), topk)

    r0 = pl.multiple_of(qi * qw, qw)
    tt = lax.broadcasted_iota(jnp.int32, (qw, qw), 1)
    rr = lax.broadcasted_iota(jnp.int32, (qw, qw), 0)
    own = jnp.logical_and(tt >= rr, tt // blk == rr // blk)
    s = jnp.where(own, _nt(ka_ref[pl.ds(r0, qw), :], qa_ref[...]), NEG_INF)
    cq = slope2 * (qb_lane[0:1, :] * blk).astype(F32)
    smax = jnp.max(s, axis=0, keepdims=True)
    p = jnp.exp2(s - smax)
    m0 = smax + cq
    l0 = jnp.sum(p, axis=0, keepdims=True)
    acc0 = jnp.dot(vt_ref[:, pl.ds(r0, qw)], p.astype(BF16), preferred_element_type=F32)

    def body(g, carry):
        m, l, acc = carry
        j0 = g * group
        c0 = pl.multiple_of(j0 * blk, group * blk)
        selg = sel_ref[pl.ds(pl.multiple_of(j0, group), group), :]
        s = _nt(ka_ref[pl.ds(c0, group * blk), :], qa_ref[...])
        ons, cjs = [], []
        mn = m
        for b in range(group):
            cj = slope2 * ((j0 + b) * blk).astype(F32)
            on = selg[b:b + 1, :] > 0.0
            cm = jnp.max(s[b * blk:(b + 1) * blk], axis=0, keepdims=True) + cj
            mn = jnp.maximum(mn, jnp.where(on, cm, NEG_INF))
            ons.append(on)
            cjs.append(cj)
        ps = [jnp.exp2(s[b * blk:(b + 1) * blk] - jnp.where(ons[b], mn - cjs[b], float("inf")))
              for b in range(group)]
        alpha = jnp.exp2(m - mn)
        lsum = ps[0].sum(axis=0, keepdims=True)
        for b in range(1, group):
            lsum = lsum + ps[b].sum(axis=0, keepdims=True)
        p = jnp.concatenate([x.astype(BF16) for x in ps], axis=0)
        pv = jnp.dot(vt_ref[:, pl.ds(c0, group * blk)], p, preferred_element_type=F32)
        return mn, l * alpha + lsum, acc * alpha + pv

    n_groups = (MOBA_QB * qi + MOBA_QB - 1 + group - 1) // group
    _, l, acc = lax.fori_loop(0, n_groups, body, (m0, l0, acc0))
    o_ref[...] = (acc / l).T


def moba_prompt(z, k, v, slopes):
    t = k.shape[0]
    nb = t // MOBA_BLOCK
    topk = min(MOBA_TOPK, nb - 1)
    group = next(g for g in (MOBA_GROUP, 4, 2, 1) if nb % g == 0)
    qw = MOBA_QB * MOBA_BLOCK
    kern = functools.partial(_moba_prompt_kernel, nb=nb, topk=topk, group=group)
    grid_spec = pltpu.PrefetchScalarGridSpec(
        num_scalar_prefetch=1,
        grid=(N_ATT_HEADS, nb // MOBA_QB),
        in_specs=[pl.BlockSpec((qw, HEAD_DIM), lambda h, qi, sl: (qi, ZA + h)),
                  pl.BlockSpec((t, HEAD_DIM), lambda h, qi, sl: (0, h)),
                  pl.BlockSpec((t, HEAD_DIM), lambda h, qi, sl: (0, h))],
        out_specs=pl.BlockSpec((qw, HEAD_DIM), lambda h, qi, sl: (qi, h)),
        scratch_shapes=[pltpu.VMEM((t, 2 * HEAD_DIM), BF16),
                        pltpu.VMEM((HEAD_DIM, t), BF16),
                        pltpu.VMEM((nb, HEAD_DIM), F32),
                        pltpu.VMEM((qw, 2 * HEAD_DIM), BF16),
                        pltpu.VMEM((nb, qw), F32)],
    )
    return pl.pallas_call(
        kern,
        out_shape=jax.ShapeDtypeStruct((t, N_ATT_HEADS * HEAD_DIM), F32),
        grid_spec=grid_spec,
        compiler_params=_cparams(("arbitrary", "arbitrary")),
        name="moba_prompt",
    )(slopes, z, k, v)


def _lane_topk_mask(g, topk, n_valid):
    lane = lax.broadcasted_iota(jnp.int32, g.shape, 1).astype(F32)
    g = jnp.where(lane < n_valid, g, NEG_INF)
    sel = jnp.zeros(g.shape, F32)
    for _ in range(topk):
        mx = jnp.max(g, axis=1, keepdims=True)
        idx = jnp.min(jnp.where(g == mx, lane, float(g.shape[1])), axis=1, keepdims=True)
        hit = lane == idx
        sel = jnp.where(jnp.logical_and(hit, mx > NEG_INF), 1.0, sel)
        g = jnp.where(hit, NEG_INF, g)
    return sel


def _moba_sample_kernel(pt_ref, slope_ref, q_ref, kn_ref, vn_ref, *refs, nb, nt, past_len, bps):
    del pt_ref
    npg = bps * PAGES_PER_BLOCK
    ck_refs, cv_refs = refs[:npg], refs[npg:2 * npg]
    o_ref, qf_ref, q16_ref, bias_ref, ks_ref, m_ref, l_ref, acc_ref = refs[2 * npg:]
    blk = MOBA_BLOCK
    nh = N_ATT_HEADS
    hh = nh // 2
    nl = nh * nt
    nlh = hh * nt
    wide = blk * hh
    step = pl.program_id(1)
    slope2 = slope_ref[...] * LOG2E

    @pl.when(step == 0)
    def _():
        aq = q_ref[...] * ATT_SCALE
        qf = jnp.concatenate([aq[:, h * HEAD_DIM:(h + 1) * HEAD_DIM] for h in range(nh)], axis=0)
        qf_ref[...] = qf
        q16_ref[...] = (qf * LOG2E).astype(BF16)
        qrow = lax.broadcasted_iota(jnp.int32, (nl, wide), 0)
        kcol = lax.broadcasted_iota(jnp.int32, (nl, wide), 1)
        same_head = kcol % hh == (qrow // nt) % hh
        bias_ref[...] = jnp.where(same_head, -slope2 * (qrow % nt - kcol // hh).astype(F32), NEG_INF)
        m_ref[...] = jnp.zeros(m_ref.shape, F32)
        l_ref[...] = jnp.zeros(l_ref.shape, F32)

    lane = lax.broadcasted_iota(jnp.int32, (nl, HEAD_DIM), 1)
    m_all, l_all = m_ref[...], l_ref[...]
    for bb in range(bps):
        j = step * bps + bb
        pages = [PAGES_PER_BLOCK * bb + i for i in range(PAGES_PER_BLOCK)]
        mjs, ljs, accs = [], [], []
        for g in range(2):
            rows = slice(g * nlh, (g + 1) * nlh)
            kg = jnp.concatenate([ck_refs[i][0, :, g].reshape(PAGE_SIZE * hh, HEAD_DIM)
                                  for i in pages], axis=0)
            vg = jnp.concatenate([cv_refs[i][0, :, g].reshape(PAGE_SIZE * hh, HEAD_DIM)
                                  for i in pages], axis=0)
            ks_ref[pl.ds(pl.multiple_of(j * nh + g * hh, hh), hh), :] = jnp.sum(
                kg.reshape(blk, hh, HEAD_DIM), axis=0)
            s = _nt(q16_ref[rows, :], kg.astype(BF16)) + bias_ref[rows, :]
            mj = jnp.max(s, axis=1, keepdims=True)
            p = jnp.exp2(s - mj)
            mjs.append(mj)
            ljs.append(jnp.sum(p, axis=1, keepdims=True))
            accs.append(jnp.dot(p.astype(BF16), vg.astype(BF16), preferred_element_type=F32))
        acc_ref[j] = jnp.concatenate(accs, axis=0)
        off = (past_len - j * blk).astype(F32)
        m_all = jnp.where(lane == j, jnp.concatenate(mjs, axis=0) - slope2 * off, m_all)
        l_all = jnp.where(lane == j, jnp.concatenate(ljs, axis=0), l_all)
    m_ref[...] = m_all
    l_ref[...] = l_all

    @pl.when(step == nb // bps - 1)
    def _():
        qf = qf_ref[...]
        qrow = lax.broadcasted_iota(jnp.int32, (nl, nb * nh), 0)
        gcol = lax.broadcasted_iota(jnp.int32, (nl, nb * nh), 1)
        g_all = _nt(qf, ks_ref[...] * (1.0 / blk), precision=lax.Precision.HIGHEST)
        g_own = jnp.where(gcol % nh == qrow // nt, g_all, 0.0)
        erow = lax.broadcasted_iota(jnp.int32, (nb * nh, HEAD_DIM), 0)
        ecol = lax.broadcasted_iota(jnp.int32, (nb * nh, HEAD_DIM), 1)
        pick = jnp.where(erow // nh == ecol, 1.0, 0.0)
        gate = jnp.dot(g_own, pick, preferred_element_type=F32, precision=lax.Precision.HIGHEST)
        w = _lane_topk_mask(gate, min(MOBA_TOPK, nb), nb)

        orow = lax.broadcasted_iota(jnp.int32, (nl, nl), 0)
        ocol = lax.broadcasted_iota(jnp.int32, (nl, nl), 1)
        dist = orow % nt - ocol // nh
        ok = jnp.logical_and(ocol % nh == orow // nt, dist >= 0)
        so = jnp.where(ok, _nt(q16_ref[...], kn_ref[...].astype(BF16)) - slope2 * dist.astype(F32),
                       NEG_INF)
        mo = jnp.max(so, axis=1, keepdims=True)

        mall = m_ref[...]
        mtot = jnp.maximum(mo, jnp.max(jnp.where(w > 0.0, mall, NEG_INF), axis=1, keepdims=True))
        wj = jnp.where(w > 0.0, jnp.exp2(mall - mtot), 0.0)
        po = jnp.exp2(so - mtot)
        denom = jnp.sum(wj * l_ref[...], axis=1, keepdims=True) + jnp.sum(po, axis=1, keepdims=True)
        num = jnp.dot(po.astype(BF16), vn_ref[...].astype(BF16), preferred_element_type=F32)
        for jb in range(nb):
            num = num + wj[:, jb:jb + 1] * acc_ref[jb]
        out = num / denom
        for h in range(nh):
            o_ref[:, h * HEAD_DIM:(h + 1) * HEAD_DIM] = out[h * nt:(h + 1) * nt, :]


def moba_sample(z, k_new, v_new, cache_k, cache_v, page_table, slopes, past_len):
    bsz, n_pages = page_table.shape
    nh = N_ATT_HEADS
    nt = k_new.shape[0] // (bsz * nh)
    nb = past_len // MOBA_BLOCK
    assert n_pages == nb * PAGES_PER_BLOCK and PAGES_PER_BLOCK == 2
    assert nh * nt == HEAD_DIM and nb <= HEAD_DIM
    wd = nh * HEAD_DIM
    slope_col = jnp.repeat(slopes, nt).reshape(nh * nt, 1)
    bps = SAMPLE_BLOCKS_PER_STEP
    assert nb % bps == 0
    npg = bps * PAGES_PER_BLOCK
    kern = functools.partial(_moba_sample_kernel, nb=nb, nt=nt, past_len=past_len, bps=bps)

    def page_spec(which):
        return pl.BlockSpec((1, PAGE_SIZE, 2, nh // 2, HEAD_DIM),
                            lambda b, j, pt: (pt[b * n_pages + npg * j + which], 0, 0, 0, 0))

    new_spec = pl.BlockSpec((nt * nh, HEAD_DIM), lambda b, j, pt: (b, 0))
    grid_spec = pltpu.PrefetchScalarGridSpec(
        num_scalar_prefetch=1,
        grid=(bsz, nb // bps),
        in_specs=[pl.BlockSpec((nh * nt, 1), lambda b, j, pt: (0, 0)),
                  pl.BlockSpec((nt, wd), lambda b, j, pt: (b, ZA * HEAD_DIM // wd)),
                  new_spec, new_spec] + [page_spec(i) for i in range(npg)] * 2,
        out_specs=pl.BlockSpec((nt, wd), lambda b, j, pt: (b, 0)),
        scratch_shapes=[pltpu.VMEM((nh * nt, HEAD_DIM), F32),
                        pltpu.VMEM((nh * nt, HEAD_DIM), BF16),
                        pltpu.VMEM((nh * nt, MOBA_BLOCK * nh // 2), F32),
                        pltpu.VMEM((nb * nh, HEAD_DIM), F32),
                        pltpu.VMEM((nh * nt, HEAD_DIM), F32),
                        pltpu.VMEM((nh * nt, HEAD_DIM), F32),
                        pltpu.VMEM((nb, nh * nt, HEAD_DIM), F32)],
    )
    return pl.pallas_call(
        kern,
        out_shape=jax.ShapeDtypeStruct((bsz * nt, wd), F32),
        grid_spec=grid_spec,
        compiler_params=_cparams(("arbitrary", "arbitrary")),
        name="moba_sample",
    )(page_table.reshape(-1), slope_col, z, k_new, v_new, *([cache_k] * npg + [cache_v] * npg))


def _mix_lhs_kernel(oh_ref, g_ref, oa_ref, hn_ref, an_ref, o_ref):
    hn = hn_ref[...]
    for h in range(N_HGRN_HEADS):
        sl = slice(h * HEAD_DIM, (h + 1) * HEAD_DIM)
        o = oh_ref[:, sl]
        y = o * lax.rsqrt(jnp.mean(o * o, axis=-1, keepdims=True) + EPS) * hn
        gh = g_ref[:, sl]
        o_ref[:, sl] = (y * (gh * _sigmoid(gh))).astype(BF16)
    oa = oa_ref[...]
    ya = oa * lax.rsqrt(jnp.mean(oa * oa, axis=-1, keepdims=True) + EPS) * an_ref[...]
    o_ref[:, W_GROUP:] = ya.astype(BF16)


def mix_lhs(o_hgrn, z, o_att, hgrn_out_norm, attn_out_norm, tm):
    m = o_hgrn.shape[0]
    gsec = ZG * HEAD_DIM // W_GROUP
    return pl.pallas_call(
        _mix_lhs_kernel,
        out_shape=jax.ShapeDtypeStruct((m, 2 * W_GROUP), BF16),
        grid=(m // tm,),
        in_specs=[pl.BlockSpec((tm, W_GROUP), lambda i: (i, 0)),
                  pl.BlockSpec((tm, W_GROUP), lambda i: (i, gsec)),
                  pl.BlockSpec((tm, W_GROUP), lambda i: (i, 0)),
                  pl.BlockSpec((1, HEAD_DIM), lambda i: (0, 0)),
                  pl.BlockSpec((1, W_GROUP), lambda i: (0, 0))],
        out_specs=pl.BlockSpec((tm, 2 * W_GROUP), lambda i: (i, 0)),
        compiler_params=_cparams(("arbitrary",)),
        name="mix_lhs",
    )(o_hgrn, z, o_att, hgrn_out_norm.reshape(1, -1), attn_out_norm.reshape(1, -1))


def _mix_out_kernel(c_ref, x_ref, w_ref, pm_ref, pf_ref, x1_ref, h2_ref, *, tn):
    j = pl.program_id(1)
    nj = pl.num_programs(1)
    c0 = pl.multiple_of(j * tn, tn)
    x1_ref[:, pl.ds(c0, tn)] = jnp.dot(c_ref[...], w_ref[...], preferred_element_type=F32)

    @pl.when(j == nj - 1)
    def _():
        mix = x1_ref[...]
        x1 = x_ref[...] + mix * lax.rsqrt(jnp.mean(mix * mix, axis=-1, keepdims=True) + EPS) * pm_ref[...]
        x1_ref[...] = x1
        h2 = x1 * lax.rsqrt(jnp.mean(x1 * x1, axis=-1, keepdims=True) + EPS) * pf_ref[...]
        h2_ref[...] = h2.astype(BF16)


def mix_out(lhs, x, w_out, norm_post_mix, norm_pre_ffn, tm, tn):
    m, d = x.shape
    kern = functools.partial(_mix_out_kernel, tn=tn)
    vec = lambda n: pl.BlockSpec((1, n), lambda i, j: (0, 0))
    return pl.pallas_call(
        kern,
        out_shape=(jax.ShapeDtypeStruct((m, d), F32), jax.ShapeDtypeStruct((m, d), BF16)),
        grid=(m // tm, d // tn),
        in_specs=[pl.BlockSpec((tm, lhs.shape[1]), lambda i, j: (i, 0)),
                  pl.BlockSpec((tm, d), lambda i, j: (i, 0), pipeline_mode=pl.Buffered(1)),
                  pl.BlockSpec((lhs.shape[1], tn), lambda i, j: (0, j)),
                  vec(d), vec(d)],
        out_specs=(pl.BlockSpec((tm, d), lambda i, j: (i, 0)),
                   pl.BlockSpec((tm, d), lambda i, j: (i, 0))),
        compiler_params=_cparams(("arbitrary", "arbitrary")),
        name="mix_out",
    )(lhs, x, w_out, norm_post_mix.reshape(1, -1), norm_pre_ffn.reshape(1, -1))


def _ffn_kernel(h_ref, wg_ref, wu_ref, wd_ref, x1_ref, pn_ref, y_ref):
    f = pl.program_id(1)
    nf = pl.num_programs(1)

    @pl.when(f == 0)
    def _():
        y_ref[...] = jnp.zeros(y_ref.shape, F32)

    h = h_ref[...]
    g = jnp.dot(h, wg_ref[...], preferred_element_type=F32)
    u = jnp.dot(h, wu_ref[...], preferred_element_type=F32)
    a = (g * _sigmoid(g) * u).astype(BF16)
    y_ref[...] += jnp.dot(a, wd_ref[...], preferred_element_type=F32)

    @pl.when(f == nf - 1)
    def _():
        ff = y_ref[...]
        y_ref[...] = x1_ref[...] + ff * lax.rsqrt(jnp.mean(ff * ff, axis=-1, keepdims=True) + EPS) * pn_ref[...]


def ffn(h2, x1, w_gate, w_up, w_down, norm_post_ffn, tm, tf):
    m, d = h2.shape
    dff = w_gate.shape[1]
    once = pl.Buffered(1)
    return pl.pallas_call(
        _ffn_kernel,
        out_shape=jax.ShapeDtypeStruct((m, d), F32),
        grid=(m // tm, dff // tf),
        in_specs=[pl.BlockSpec((tm, d), lambda i, f: (i, 0), pipeline_mode=once),
                  pl.BlockSpec((d, tf), lambda i, f: (0, f)),
                  pl.BlockSpec((d, tf), lambda i, f: (0, f)),
                  pl.BlockSpec((tf, d), lambda i, f: (f, 0)),
                  pl.BlockSpec((tm, d), lambda i, f: (i, 0), pipeline_mode=once),
                  pl.BlockSpec((1, d), lambda i, f: (0, 0))],
        out_specs=pl.BlockSpec((tm, d), lambda i, f: (i, 0)),
        compiler_params=_cparams(("arbitrary", "arbitrary")),
        name="ffn",
    )(h2, w_gate, w_up, w_down, x1, norm_post_ffn.reshape(1, -1))


def _alibi_slopes(n):
    return 2.0 ** (-8.0 * jnp.arange(1, n + 1, dtype=F32) / n)


def _token_tile(m, cap):
    return cap if m % cap == 0 else m


def _project(x2d, norm_pre_mix, w_in16):
    m = x2d.shape[0]
    h = rmsnorm_bf16(x2d, norm_pre_mix, _token_tile(m, 256))
    tm = _token_tile(m, 1024)
    z = matmul_cols(h, w_in16, 0, 5 * W_GROUP, tm, 1024)
    k = matmul_cols(h, w_in16, 5 * W_GROUP, W_GROUP, tm, 1024)
    v = matmul_cols(h, w_in16, 6 * W_GROUP, W_GROUP, tm, 1024)
    return z, k, v


def _finish(x2d, o_hgrn, z, o_att, w16, norms):
    hgrn_out_norm, attn_out_norm, norm_post_mix, norm_pre_ffn, norm_post_ffn = norms
    w_out16, w_gate16, w_up16, w_down16 = w16
    m = x2d.shape[0]
    lhs = mix_lhs(o_hgrn, z, o_att, hgrn_out_norm, attn_out_norm, _token_tile(m, 256))
    x1, h2 = mix_out(lhs, x2d, w_out16, norm_post_mix, norm_pre_ffn, _token_tile(m, 512), 512)
    return ffn(h2, x1, w_gate16, w_up16, w_down16, norm_post_ffn, _token_tile(m, 512), 256)


def kernel(x_prompt, x_sample, cache_k, cache_v, page_table, state_hgrn, norm_pre_mix, w_in,
           hgrn_lb_logits, hgrn_out_norm, attn_out_norm, w_out, norm_post_mix, norm_pre_ffn,
           w_gate, w_up, w_down, norm_post_ffn):
    bp, tp, d = x_prompt.shape
    bs, ts, _ = x_sample.shape
    assert bp == 1
    n_pool, page, h_att, hd = cache_k.shape
    past_len = page_table.shape[1] * page
    slopes = _alibi_slopes(N_ATT_HEADS)
    w_in16, w_out16, w_gate16, w_up16, w_down16 = (
        w.astype(BF16) for w in (w_in, w_out, w_gate, w_up, w_down))
    w16 = (w_out16, w_gate16, w_up16, w_down16)
    norms = (hgrn_out_norm, attn_out_norm, norm_post_mix, norm_pre_ffn, norm_post_ffn)

    xp = x_prompt.reshape(tp, d)
    zp, kp, vp = _project(xp, norm_pre_mix, w_in16)
    s0 = jnp.zeros((bp,) + state_hgrn.shape[1:], state_hgrn.dtype)
    oh_p, state_prompt = hgrn2(zp, hgrn_lb_logits, s0, HEAD_DIM, HEAD_DIM, 4)
    oa_p = moba_prompt(zp, kp, vp, slopes)
    y_prompt = _finish(xp, oh_p, zp, oa_p, w16, norms)

    xs = x_sample.reshape(bs * ts, d)
    zs, ks, vs = _project(xs, norm_pre_mix, w_in16)
    oh_s, state_sample = hgrn2(zs, hgrn_lb_logits, state_hgrn, ts, HGRN_SUB, 4)
    oa_s = moba_sample(zs, ks.reshape(bs * ts * h_att, hd), vs.reshape(bs * ts * h_att, hd),
                       cache_k.reshape(n_pool, page, 2, h_att // 2, hd),
                       cache_v.reshape(n_pool, page, 2, h_att // 2, hd), page_table, slopes, past_len)
    y_sample = _finish(xs, oh_s, zs, oa_s, w16, norms)

    return (y_prompt.reshape(bp, tp, d), y_sample.reshape(bs, ts, d),
            kp.reshape(bp, tp, h_att, hd), vp.reshape(bp, tp, h_att, hd),
            ks.reshape(bs, ts, h_att, hd), vs.reshape(bs, ts, h_att, hd),
            state_prompt, state_sample)
```

```python
import functools

import jax
import jax.numpy as jnp
from jax import lax
from jax.experimental import pallas as pl
from jax.experimental.pallas import tpu as pltpu

F32 = jnp.float32
BF16 = jnp.bfloat16

EPS = 1e-6
HEAD_DIM = 128
N_HGRN_HEADS = 16
N_ATT_HEADS = 16
W_GROUP = N_HGRN_HEADS * HEAD_DIM
MOBA_BLOCK = 256
MOBA_TOPK = 3
PAGE_SIZE = 128
PAGES_PER_BLOCK = MOBA_BLOCK // PAGE_SIZE
ATT_SCALE = HEAD_DIM ** -0.5
HGRN_SUB = 16
NEG_INF = float("-inf")
LOG2E = 1.4426950408889634
M_FLOOR = -1e30
MOBA_QB = 4
SAMPLE_BLOCKS_PER_STEP = 4
MOBA_GROUP = 8
VMEM_LIMIT = 56 * 1024 * 1024

ZQ, ZF, ZI, ZG, ZA = 0, 16, 32, 48, 64


def _cparams(sem):
    return pltpu.CompilerParams(dimension_semantics=sem, vmem_limit_bytes=VMEM_LIMIT)


def _nt(a, b, precision=None):
    return lax.dot_general(a, b, (((1,), (1,)), ((), ())), preferred_element_type=F32,
                           precision=precision)


def _sigmoid(x):
    return 1.0 / (1.0 + jnp.exp(-x))


def _norm_kernel(x_ref, g_ref, o_ref):
    x = x_ref[...]
    y = x * lax.rsqrt(jnp.mean(x * x, axis=-1, keepdims=True) + EPS)
    o_ref[...] = (y * g_ref[...]).astype(BF16)


def rmsnorm_bf16(x, g, tm):
    m, d = x.shape
    return pl.pallas_call(
        _norm_kernel,
        out_shape=jax.ShapeDtypeStruct((m, d), BF16),
        grid=(m // tm,),
        in_specs=[pl.BlockSpec((tm, d), lambda i: (i, 0)), pl.BlockSpec((1, d), lambda i: (0, 0))],
        out_specs=pl.BlockSpec((tm, d), lambda i: (i, 0)),
        compiler_params=_cparams(("arbitrary",)),
        name="rmsnorm_bf16",
    )(x, g.reshape(1, d))


def _mm_kernel(a_ref, w_ref, o_ref):
    o_ref[...] = jnp.dot(a_ref[...], w_ref[...], preferred_element_type=F32)


def matmul_cols(a, w, col0, n, tm, tn):
    m, k = a.shape
    cb = col0 // tn
    return pl.pallas_call(
        _mm_kernel,
        out_shape=jax.ShapeDtypeStruct((m, n), F32),
        grid=(m // tm, n // tn),
        in_specs=[pl.BlockSpec((tm, k), lambda i, j: (i, 0)),
                  pl.BlockSpec((k, tn), lambda i, j: (0, cb + j))],
        out_specs=pl.BlockSpec((tm, tn), lambda i, j: (i, j)),
        compiler_params=_cparams(("arbitrary", "arbitrary")),
        name="matmul_cols",
    )(a, w)


def _cumsum_rows(g):
    c = g.shape[0]
    row = lax.broadcasted_iota(jnp.int32, g.shape, 0)
    b = g
    sh = 1
    while sh < c:
        b = b + jnp.where(row >= sh, pltpu.roll(b, sh, 0), 0.0)
        sh *= 2
    return b


def _hgrn_kernel(q_ref, f_ref, i_ref, lbl_ref, s0_ref, o_ref, sout_ref, st_ref, *, rows_in, rows, hp):
    c = pl.program_id(2)
    nc = pl.num_programs(2)
    for hh in range(hp):
        cols = slice(hh * HEAD_DIM, (hh + 1) * HEAD_DIM)

        @pl.when(c == 0)
        def _(hh=hh):
            st_ref[hh] = s0_ref[0, hh].T

        o, st_new = _hgrn_chunk(q_ref[:, cols], f_ref[:, cols], i_ref[:, cols], lbl_ref[:, cols],
                                st_ref[hh], rows_in, rows)
        o_ref[:, cols] = o
        st_ref[hh] = st_new

        @pl.when(c == nc - 1)
        def _(hh=hh, st_new=st_new):
            sout_ref[0, hh] = st_new.T


def _hgrn_chunk(hq, hf, v, lg, st, rows_in, rows):
    lge = jnp.exp(lg - jnp.max(lg, axis=0, keepdims=True))
    lb = lge[0:1, :] / jnp.sum(lge, axis=0, keepdims=True)

    f = lb + (1.0 - lb) * _sigmoid(hf)
    q = hq * _sigmoid(hq)
    k = 1.0 - f
    g = jnp.log(f)
    if rows > rows_in:
        pad = jnp.zeros((rows - rows_in, HEAD_DIM), F32)
        q, k, g, v = (jnp.concatenate([a, pad], axis=0) for a in (q, k, g, v))
    b = _cumsum_rows(g)

    m = HGRN_SUB
    ns = rows // m

    o = _nt((q * jnp.exp(b)).astype(BF16), st.astype(BF16))

    def exact_diag(width):
        row_w = lax.broadcasted_iota(jnp.int32, (width, HEAD_DIM), 0)
        outs = []
        for i in range(rows // width):
            sl = slice(i * width, (i + 1) * width)
            qi, ki, bi, vi = q[sl], k[sl], b[sl], v[sl]
            oi = jnp.zeros((width, HEAD_DIM), F32)
            for s in range(width):
                e = jnp.exp(jnp.where(row_w >= s, bi - bi[s:s + 1], NEG_INF))
                a = jnp.sum(qi * e * ki[s:s + 1], axis=-1, keepdims=True)
                oi = oi + a * vi[s:s + 1]
            outs.append(oi)
        return outs

    half = m // 2
    if ns >= 2:
        o_diag = exact_diag(half)
        q3, k3, b3, v3 = (a.reshape(ns, m, HEAD_DIM) for a in (q, k, b, v))
        r = b3[:, half - 1:half, :]
        qt = (q3[:, half:, :] * jnp.exp(b3[:, half:, :] - r)).reshape(ns * half, HEAD_DIM)
        kt = (k3[:, :half, :] * jnp.exp(r - b3[:, :half, :])).reshape(ns * half, HEAD_DIM)
        a = _nt(qt.astype(BF16), kt.astype(BF16))
        arow = lax.broadcasted_iota(jnp.int32, a.shape, 0) // half
        acol = lax.broadcasted_iota(jnp.int32, a.shape, 1) // half
        a = jnp.where(arow == acol, a, 0.0)
        low = jnp.dot(a.astype(BF16), v3[:, :half, :].reshape(ns * half, HEAD_DIM).astype(BF16),
                      preferred_element_type=F32)
        for i in range(ns):
            o_diag[2 * i + 1] = o_diag[2 * i + 1] + low[i * half:(i + 1) * half]
    else:
        o_diag = exact_diag(m)
    o = o + jnp.concatenate(o_diag, axis=0) if len(o_diag) > 1 else o + o_diag[0]

    for j in range(ns - 1):
        lo = (j + 1) * m
        r = b[lo - 1:lo]
        kt = k[j * m:lo] * jnp.exp(r - b[j * m:lo])
        qt = q[lo:] * jnp.exp(b[lo:] - r)
        a = _nt(qt.astype(BF16), kt.astype(BF16))
        contrib = jnp.dot(a.astype(BF16), v[j * m:lo].astype(BF16), preferred_element_type=F32)
        o = o + jnp.concatenate([jnp.zeros((lo, HEAD_DIM), F32), contrib], axis=0)

    b_last = b[rows - 1:rows]
    kdec = k * jnp.exp(b_last - b)
    vp, kp = v, kdec
    if rows < HEAD_DIM:
        zp = jnp.zeros((HEAD_DIM - rows, HEAD_DIM), F32)
        vp, kp = jnp.concatenate([v, zp], axis=0), jnp.concatenate([kdec, zp], axis=0)
    st_new = st * jnp.exp(b_last) + jnp.dot(vp.T.astype(BF16), kp.astype(BF16),
                                             preferred_element_type=F32)
    return o[:rows_in], st_new


def hgrn2(z, lb_logits, s0, rows_in, rows, hp):
    bsz = s0.shape[0]
    t = z.shape[0] // bsz
    nc = t // rows_in
    kern = functools.partial(_hgrn_kernel, rows_in=rows_in, rows=rows, hp=hp)
    wide = hp * HEAD_DIM

    def zspec(off):
        return pl.BlockSpec((rows_in, wide), lambda b, h, c: (b * nc + c, off // hp + h))

    sspec = pl.BlockSpec((1, hp, HEAD_DIM, HEAD_DIM), lambda b, h, c: (b, h, 0, 0))
    return pl.pallas_call(
        kern,
        out_shape=(jax.ShapeDtypeStruct((bsz * t, W_GROUP), F32),
                   jax.ShapeDtypeStruct(s0.shape, s0.dtype)),
        grid=(bsz, N_HGRN_HEADS // hp, nc),
        in_specs=[zspec(ZQ), zspec(ZF), zspec(ZI),
                  pl.BlockSpec((lb_logits.shape[0], wide), lambda b, h, c: (0, h)),
                  sspec],
        out_specs=(pl.BlockSpec((rows_in, wide), lambda b, h, c: (b * nc + c, h)), sspec),
        scratch_shapes=[pltpu.VMEM((hp, HEAD_DIM, HEAD_DIM), F32)],
        compiler_params=_cparams(("arbitrary", "arbitrary", "arbitrary")),
        name="hgrn2",
    )(z, z, z, lb_logits, s0)


def _topk_rows_mask(g, topk):
    nb = g.shape[0]
    jj = lax.broadcasted_iota(jnp.int32, g.shape, 0)
    sel = jnp.zeros(g.shape, F32)
    for _ in range(topk):
        mx = jnp.max(g, axis=0, keepdims=True)
        idx = jnp.min(jnp.where(g == mx, jj, nb), axis=0, keepdims=True)
        hit = jj == idx
        sel = jnp.where(jnp.logical_and(hit, mx > NEG_INF), 1.0, sel)
        g = jnp.where(hit, NEG_INF, g)
    return sel


def _split3_bf16(x):
    a = x.astype(BF16)
    r = x - a.astype(F32)
    b = r.astype(BF16)
    return a, b, (r - b.astype(F32)).astype(BF16)


def _moba_prompt_kernel(slope_ref, q_ref, k_ref, v_ref, o_ref,
                        ka_ref, vt_ref, kmean_ref, qa_ref, sel_ref, *, nb, topk, group):
    blk = MOBA_BLOCK
    qw = MOBA_QB * blk
    h = pl.program_id(0)
    qi = pl.program_id(1)
    slope2 = slope_ref[h] * LOG2E

    @pl.when(qi == 0)
    def _():
        lane = lax.broadcasted_iota(jnp.int32, (blk, HEAD_DIM), 1)
        rloc = lax.broadcasted_iota(jnp.int32, (blk, HEAD_DIM), 0).astype(F32)
        b1, b2, b3 = _split3_bf16(slope2 * rloc)
        aug = jnp.where(lane == 0, b1.astype(F32),
                        jnp.where(lane == 1, b2.astype(F32),
                                  jnp.where(lane == 2, b3.astype(F32), 0.0))).astype(BF16)

        def prep(j, carry):
            r0 = pl.multiple_of(j * blk, blk)
            kj = k_ref[pl.ds(r0, blk), :]
            ka_ref[pl.ds(r0, blk), :] = jnp.concatenate([kj.astype(BF16), aug], axis=1)
            vt_ref[:, pl.ds(r0, blk)] = v_ref[pl.ds(r0, blk), :].T.astype(BF16)
            rowj = lax.broadcasted_iota(jnp.int32, (nb, HEAD_DIM), 0)
            kmean_ref[...] = jnp.where(rowj == j, jnp.mean(kj, axis=0, keepdims=True), kmean_ref[...])
            return carry
        lax.fori_loop(0, nb, prep, 0)

    q = q_ref[...] * ATT_SCALE
    lane_q = lax.broadcasted_iota(jnp.int32, (qw, HEAD_DIM), 1)
    ones = jnp.where(lane_q < 3, 1.0, 0.0).astype(BF16)
    qa_ref[...] = jnp.concatenate([(q * LOG2E).astype(BF16), ones], axis=1)

    gate = _nt(kmean_ref[...], q, precision=lax.Precision.HIGHEST)
    jj = lax.broadcasted_iota(jnp.int32, (nb, qw), 0)
    qb_lane = MOBA_QB * qi + lax.broadcasted_iota(jnp.int32, (nb, qw), 1) // blk
    sel_ref[...] = _topk_rows_mask(jnp.where(jj < qb_lane, gate, NEG_INF), topk)

    qb_row = qb_lane[0:1, :]
    t_loc = lax.broadcasted_iota(jnp.int32, (blk, qw), 1) % blk
    r_loc = lax.broadcasted_iota(jnp.int32, (blk, qw), 0)
    causal_bias = jnp.where(t_loc >= r_loc, 0.0, NEG_INF)

    def body(g, carry, diag):
        m, l, acc = carry
        j0 = g * group
        c0 = pl.multiple_of(j0 * blk, group * blk)
        selg = sel_ref[pl.ds(pl.multiple_of(j0, group), group), :]
        s = _nt(ka_ref[pl.ds(c0, group * blk), :], qa_ref[...])
        ons, cjs, sbs = [], [], []
        mn = m
        for b in range(group):
            cj = slope2 * ((j0 + b) * blk).astype(F32)
            on = selg[b:b + 1, :] > 0.0
            sb = s[b * blk:(b + 1) * blk]
            if diag:
                own = qb_row == j0 + b
                sb = sb + jnp.where(own, causal_bias, jnp.where(on, 0.0, NEG_INF))
                on = jnp.where(own, 1.0, selg[b:b + 1, :]) > 0.0
            cm = jnp.max(sb, axis=0, keepdims=True) + cj
            mn = jnp.maximum(mn, jnp.where(on, cm, NEG_INF))
            ons.append(on)
            cjs.append(cj)
            sbs.append(sb)
        ps = [jnp.exp2(sbs[b] - jnp.where(ons[b], mn - cjs[b], float("inf"))) for b in range(group)]
        alpha = jnp.exp2(m - mn)
        lsum = ps[0].sum(axis=0, keepdims=True)
        for b in range(1, group):
            lsum = lsum + ps[b].sum(axis=0, keepdims=True)
        p = jnp.concatenate([x.astype(BF16) for x in ps], axis=0)
        pv = jnp.dot(vt_ref[:, pl.ds(c0, group * blk)], p, preferred_element_type=F32)
        return mn, l * alpha + lsum, acc * alpha + pv

    n_groups = (MOBA_QB * qi + MOBA_QB - 1 + group - 1) // group
    init = (jnp.full((1, qw), M_FLOOR, F32), jnp.zeros((1, qw), F32), jnp.zeros((HEAD_DIM, qw), F32))
    carry = lax.fori_loop(0, n_groups - 1, functools.partial(body, diag=False), init)
    _, l, acc = body(n_groups - 1, carry, True)
    o_ref[...] = (acc / l).T


def moba_prompt(z, k, v, slopes):
    t = k.shape[0]
    nb = t // MOBA_BLOCK
    topk = min(MOBA_TOPK, nb - 1)
    group = next(g for g in (MOBA_GROUP, 4, 2, 1) if nb % g == 0)
    qw = MOBA_QB * MOBA_BLOCK
    kern = functools.partial(_moba_prompt_kernel, nb=nb, topk=topk, group=group)
    grid_spec = pltpu.PrefetchScalarGridSpec(
        num_scalar_prefetch=1,
        grid=(N_ATT_HEADS, nb // MOBA_QB),
        in_specs=[pl.BlockSpec((qw, HEAD_DIM), lambda h, qi, sl: (qi, ZA + h)),
                  pl.BlockSpec((t, HEAD_DIM), lambda h, qi, sl: (0, h)),
                  pl.BlockSpec((t, HEAD_DIM), lambda h, qi, sl: (0, h))],
        out_specs=pl.BlockSpec((qw, HEAD_DIM), lambda h, qi, sl: (qi, h)),
        scratch_shapes=[pltpu.VMEM((t, 2 * HEAD_DIM), BF16),
                        pltpu.VMEM((HEAD_DIM, t), BF16),
                        pltpu.VMEM((nb, HEAD_DIM), F32),
                        pltpu.VMEM((qw, 2 * HEAD_DIM), BF16),
                        pltpu.VMEM((nb, qw), F32)],
    )
    return pl.pallas_call(
        kern,
        out_shape=jax.ShapeDtypeStruct((t, N_ATT_HEADS * HEAD_DIM), F32),
        grid_spec=grid_spec,
        compiler_params=_cparams(("arbitrary", "arbitrary")),
        name="moba_prompt",
    )(slopes, z, k, v)


def _lane_topk_mask(g, topk, n_valid):
    lane = lax.broadcasted_iota(jnp.int32, g.shape, 1).astype(F32)
    g = jnp.where(lane < n_valid, g, NEG_INF)
    sel = jnp.zeros(g.shape, F32)
    for _ in range(topk):
        mx = jnp.max(g, axis=1, keepdims=True)
        idx = jnp.min(jnp.where(g == mx, lane, float(g.shape[1])), axis=1, keepdims=True)
        hit = lane == idx
        sel = jnp.where(jnp.logical_and(hit, mx > NEG_INF), 1.0, sel)
        g = jnp.where(hit, NEG_INF, g)
    return sel


def _moba_sample_kernel(pt_ref, slope_ref, q_ref, kn_ref, vn_ref, *refs, nb, nt, past_len, bps):
    del pt_ref
    npg = bps * PAGES_PER_BLOCK
    ck_refs, cv_refs = refs[:npg], refs[npg:2 * npg]
    o_ref, qf_ref, q16_ref, bias_ref, ks_ref, m_ref, l_ref, acc_ref = refs[2 * npg:]
    blk = MOBA_BLOCK
    nh = N_ATT_HEADS
    hh = nh // 2
    nl = nh * nt
    nlh = hh * nt
    wide = blk * hh
    step = pl.program_id(1)
    slope2 = slope_ref[...] * LOG2E

    @pl.when(step == 0)
    def _():
        aq = q_ref[...] * ATT_SCALE
        qf = jnp.concatenate([aq[:, h * HEAD_DIM:(h + 1) * HEAD_DIM] for h in range(nh)], axis=0)
        qf_ref[...] = qf
        q16_ref[...] = (qf * LOG2E).astype(BF16)
        qrow = lax.broadcasted_iota(jnp.int32, (nl, wide), 0)
        kcol = lax.broadcasted_iota(jnp.int32, (nl, wide), 1)
        same_head = kcol % hh == (qrow // nt) % hh
        bias_ref[...] = jnp.where(same_head, -slope2 * (qrow % nt - kcol // hh).astype(F32), NEG_INF)
        m_ref[...] = jnp.zeros(m_ref.shape, F32)
        l_ref[...] = jnp.zeros(l_ref.shape, F32)

    lane = lax.broadcasted_iota(jnp.int32, (nl, HEAD_DIM), 1)
    m_all, l_all = m_ref[...], l_ref[...]
    for bb in range(bps):
        j = step * bps + bb
        pages = [PAGES_PER_BLOCK * bb + i for i in range(PAGES_PER_BLOCK)]
        mjs, ljs, accs = [], [], []
        for g in range(2):
            rows = slice(g * nlh, (g + 1) * nlh)
            kg = jnp.concatenate([ck_refs[i][0, :, g].reshape(PAGE_SIZE * hh, HEAD_DIM)
                                  for i in pages], axis=0)
            vg = jnp.concatenate([cv_refs[i][0, :, g].reshape(PAGE_SIZE * hh, HEAD_DIM)
                                  for i in pages], axis=0)
            ks_ref[pl.ds(pl.multiple_of(j * nh + g * hh, hh), hh), :] = jnp.sum(
                kg.reshape(blk, hh, HEAD_DIM), axis=0)
            s = _nt(q16_ref[rows, :], kg.astype(BF16)) + bias_ref[rows, :]
            mj = jnp.max(s, axis=1, keepdims=True)
            p = jnp.exp2(s - mj)
            mjs.append(mj)
            ljs.append(jnp.sum(p, axis=1, keepdims=True))
            accs.append(jnp.dot(p.astype(BF16), vg.astype(BF16), preferred_element_type=F32))
        acc_ref[j] = jnp.concatenate(accs, axis=0)
        off = (past_len - j * blk).astype(F32)
        m_all = jnp.where(lane == j, jnp.concatenate(mjs, axis=0) - slope2 * off, m_all)
        l_all = jnp.where(lane == j, jnp.concatenate(ljs, axis=0), l_all)
    m_ref[...] = m_all
    l_ref[...] = l_all

    @pl.when(step == nb // bps - 1)
    def _():
        qf = qf_ref[...]
        qrow = lax.broadcasted_iota(jnp.int32, (nl, nb * nh), 0)
        gcol = lax.broadcasted_iota(jnp.int32, (nl, nb * nh), 1)
        g_all = _nt(qf, ks_ref[...] * (1.0 / blk), precision=lax.Precision.HIGHEST)
        g_own = jnp.where(gcol % nh == qrow // nt, g_all, 0.0)
        erow = lax.broadcasted_iota(jnp.int32, (nb * nh, HEAD_DIM), 0)
        ecol = lax.broadcasted_iota(jnp.int32, (nb * nh, HEAD_DIM), 1)
        pick = jnp.where(erow // nh == ecol, 1.0, 0.0)
        gate = jnp.dot(g_own, pick, preferred_element_type=F32, precision=lax.Precision.HIGHEST)
        w = _lane_topk_mask(gate, min(MOBA_TOPK, nb), nb)

        orow = lax.broadcasted_iota(jnp.int32, (nl, nl), 0)
        ocol = lax.broadcasted_iota(jnp.int32, (nl, nl), 1)
        dist = orow % nt - ocol // nh
        ok = jnp.logical_and(ocol % nh == orow // nt, dist >= 0)
        so = jnp.where(ok, _nt(q16_ref[...], kn_ref[...].astype(BF16)) - slope2 * dist.astype(F32),
                       NEG_INF)
        mo = jnp.max(so, axis=1, keepdims=True)

        mall = m_ref[...]
        mtot = jnp.maximum(mo, jnp.max(jnp.where(w > 0.0, mall, NEG_INF), axis=1, keepdims=True))
        wj = jnp.where(w > 0.0, jnp.exp2(mall - mtot), 0.0)
        po = jnp.exp2(so - mtot)
        denom = jnp.sum(wj * l_ref[...], axis=1, keepdims=True) + jnp.sum(po, axis=1, keepdims=True)
        num = jnp.dot(po.astype(BF16), vn_ref[...].astype(BF16), preferred_element_type=F32)
        for jb in range(nb):
            num = num + wj[:, jb:jb + 1] * acc_ref[jb]
        out = num / denom
        for h in range(nh):
            o_ref[:, h * HEAD_DIM:(h + 1) * HEAD_DIM] = out[h * nt:(h + 1) * nt, :]


def moba_sample(z, k_new, v_new, cache_k, cache_v, page_table, slopes, past_len):
    bsz, n_pages = page_table.shape
    nh = N_ATT_HEADS
    nt = k_new.shape[0] // (bsz * nh)
    nb = past_len // MOBA_BLOCK
    assert n_pages == nb * PAGES_PER_BLOCK and PAGES_PER_BLOCK == 2
    assert nh * nt == HEAD_DIM and nb <= HEAD_DIM
    wd = nh * HEAD_DIM
    slope_col = jnp.repeat(slopes, nt).reshape(nh * nt, 1)
    bps = SAMPLE_BLOCKS_PER_STEP
    assert nb % bps == 0
    npg = bps * PAGES_PER_BLOCK
    kern = functools.partial(_moba_sample_kernel, nb=nb, nt=nt, past_len=past_len, bps=bps)

    def page_spec(which):
        return pl.BlockSpec((1, PAGE_SIZE, 2, nh // 2, HEAD_DIM),
                            lambda b, j, pt: (pt[b * n_pages + npg * j + which], 0, 0, 0, 0))

    new_spec = pl.BlockSpec((nt * nh, HEAD_DIM), lambda b, j, pt: (b, 0))
    grid_spec = pltpu.PrefetchScalarGridSpec(
        num_scalar_prefetch=1,
        grid=(bsz, nb // bps),
        in_specs=[pl.BlockSpec((nh * nt, 1), lambda b, j, pt: (0, 0)),
                  pl.BlockSpec((nt, wd), lambda b, j, pt: (b, ZA * HEAD_DIM // wd)),
                  new_spec, new_spec] + [page_spec(i) for i in range(npg)] * 2,
        out_specs=pl.BlockSpec((nt, wd), lambda b, j, pt: (b, 0)),
        scratch_shapes=[pltpu.VMEM((nh * nt, HEAD_DIM), F32),
                        pltpu.VMEM((nh * nt, HEAD_DIM), BF16),
                        pltpu.VMEM((nh * nt, MOBA_BLOCK * nh // 2), F32),
                        pltpu.VMEM((nb * nh, HEAD_DIM), F32),
                        pltpu.VMEM((nh * nt, HEAD_DIM), F32),
                        pltpu.VMEM((nh * nt, HEAD_DIM), F32),
                        pltpu.VMEM((nb, nh * nt, HEAD_DIM), F32)],
    )
    return pl.pallas_call(
        kern,
        out_shape=jax.ShapeDtypeStruct((bsz * nt, wd), F32),
        grid_spec=grid_spec,
        compiler_params=_cparams(("arbitrary", "arbitrary")),
        name="moba_sample",
    )(page_table.reshape(-1), slope_col, z, k_new, v_new, *([cache_k] * npg + [cache_v] * npg))


def _mix_lhs_kernel(oh_ref, g_ref, oa_ref, hn_ref, an_ref, o_ref):
    hn = hn_ref[...]
    for h in range(N_HGRN_HEADS):
        sl = slice(h * HEAD_DIM, (h + 1) * HEAD_DIM)
        o = oh_ref[:, sl]
        y = o * lax.rsqrt(jnp.mean(o * o, axis=-1, keepdims=True) + EPS) * hn
        gh = g_ref[:, sl]
        o_ref[:, sl] = (y * (gh * _sigmoid(gh))).astype(BF16)
    oa = oa_ref[...]
    ya = oa * lax.rsqrt(jnp.mean(oa * oa, axis=-1, keepdims=True) + EPS) * an_ref[...]
    o_ref[:, W_GROUP:] = ya.astype(BF16)


def mix_lhs(o_hgrn, z, o_att, hgrn_out_norm, attn_out_norm, tm):
    m = o_hgrn.shape[0]
    gsec = ZG * HEAD_DIM // W_GROUP
    return pl.pallas_call(
        _mix_lhs_kernel,
        out_shape=jax.ShapeDtypeStruct((m, 2 * W_GROUP), BF16),
        grid=(m // tm,),
        in_specs=[pl.BlockSpec((tm, W_GROUP), lambda i: (i, 0)),
                  pl.BlockSpec((tm, W_GROUP), lambda i: (i, gsec)),
                  pl.BlockSpec((tm, W_GROUP), lambda i: (i, 0)),
                  pl.BlockSpec((1, HEAD_DIM), lambda i: (0, 0)),
                  pl.BlockSpec((1, W_GROUP), lambda i: (0, 0))],
        out_specs=pl.BlockSpec((tm, 2 * W_GROUP), lambda i: (i, 0)),
        compiler_params=_cparams(("arbitrary",)),
        name="mix_lhs",
    )(o_hgrn, z, o_att, hgrn_out_norm.reshape(1, -1), attn_out_norm.reshape(1, -1))


def _mix_out_kernel(c_ref, x_ref, w_ref, pm_ref, pf_ref, x1_ref, h2_ref, *, tn):
    j = pl.program_id(1)
    nj = pl.num_programs(1)
    c0 = pl.multiple_of(j * tn, tn)
    x1_ref[:, pl.ds(c0, tn)] = jnp.dot(c_ref[...], w_ref[...], preferred_element_type=F32)

    @pl.when(j == nj - 1)
    def _():
        mix = x1_ref[...]
        x1 = x_ref[...] + mix * lax.rsqrt(jnp.mean(mix * mix, axis=-1, keepdims=True) + EPS) * pm_ref[...]
        x1_ref[...] = x1
        h2 = x1 * lax.rsqrt(jnp.mean(x1 * x1, axis=-1, keepdims=True) + EPS) * pf_ref[...]
        h2_ref[...] = h2.astype(BF16)


def mix_out(lhs, x, w_out, norm_post_mix, norm_pre_ffn, tm, tn):
    m, d = x.shape
    kern = functools.partial(_mix_out_kernel, tn=tn)
    vec = lambda n: pl.BlockSpec((1, n), lambda i, j: (0, 0))
    return pl.pallas_call(
        kern,
        out_shape=(jax.ShapeDtypeStruct((m, d), F32), jax.ShapeDtypeStruct((m, d), BF16)),
        grid=(m // tm, d // tn),
        in_specs=[pl.BlockSpec((tm, lhs.shape[1]), lambda i, j: (i, 0)),
                  pl.BlockSpec((tm, d), lambda i, j: (i, 0), pipeline_mode=pl.Buffered(1)),
                  pl.BlockSpec((lhs.shape[1], tn), lambda i, j: (0, j)),
                  vec(d), vec(d)],
        out_specs=(pl.BlockSpec((tm, d), lambda i, j: (i, 0)),
                   pl.BlockSpec((tm, d), lambda i, j: (i, 0))),
        compiler_params=_cparams(("arbitrary", "arbitrary")),
        name="mix_out",
    )(lhs, x, w_out, norm_post_mix.reshape(1, -1), norm_pre_ffn.reshape(1, -1))


def _ffn_kernel(h_ref, wg_ref, wu_ref, wd_ref, x1_ref, pn_ref, y_ref):
    f = pl.program_id(1)
    nf = pl.num_programs(1)

    @pl.when(f == 0)
    def _():
        y_ref[...] = jnp.zeros(y_ref.shape, F32)

    h = h_ref[...]
    g = jnp.dot(h, wg_ref[...], preferred_element_type=F32)
    u = jnp.dot(h, wu_ref[...], preferred_element_type=F32)
    a = (g * _sigmoid(g) * u).astype(BF16)
    y_ref[...] += jnp.dot(a, wd_ref[...], preferred_element_type=F32)

    @pl.when(f == nf - 1)
    def _():
        ff = y_ref[...]
        y_ref[...] = x1_ref[...] + ff * lax.rsqrt(jnp.mean(ff * ff, axis=-1, keepdims=True) + EPS) * pn_ref[...]


def ffn(h2, x1, w_gate, w_up, w_down, norm_post_ffn, tm, tf):
    m, d = h2.shape
    dff = w_gate.shape[1]
    once = pl.Buffered(1)
    return pl.pallas_call(
        _ffn_kernel,
        out_shape=jax.ShapeDtypeStruct((m, d), F32),
        grid=(m // tm, dff // tf),
        in_specs=[pl.BlockSpec((tm, d), lambda i, f: (i, 0), pipeline_mode=once),
                  pl.BlockSpec((d, tf), lambda i, f: (0, f)),
                  pl.BlockSpec((d, tf), lambda i, f: (0, f)),
                  pl.BlockSpec((tf, d), lambda i, f: (f, 0)),
                  pl.BlockSpec((tm, d), lambda i, f: (i, 0), pipeline_mode=once),
                  pl.BlockSpec((1, d), lambda i, f: (0, 0))],
        out_specs=pl.BlockSpec((tm, d), lambda i, f: (i, 0)),
        compiler_params=_cparams(("arbitrary", "arbitrary")),
        name="ffn",
    )(h2, w_gate, w_up, w_down, x1, norm_post_ffn.reshape(1, -1))


def _alibi_slopes(n):
    return 2.0 ** (-8.0 * jnp.arange(1, n + 1, dtype=F32) / n)


def _token_tile(m, cap):
    return cap if m % cap == 0 else m


def _project(x2d, norm_pre_mix, w_in16):
    m = x2d.shape[0]
    h = rmsnorm_bf16(x2d, norm_pre_mix, _token_tile(m, 256))
    tm = _token_tile(m, 1024)
    z = matmul_cols(h, w_in16, 0, 5 * W_GROUP, tm, 1024)
    k = matmul_cols(h, w_in16, 5 * W_GROUP, W_GROUP, tm, 1024)
    v = matmul_cols(h, w_in16, 6 * W_GROUP, W_GROUP, tm, 1024)
    return z, k, v


def _finish(x2d, o_hgrn, z, o_att, w16, norms):
    hgrn_out_norm, attn_out_norm, norm_post_mix, norm_pre_ffn, norm_post_ffn = norms
    w_out16, w_gate16, w_up16, w_down16 = w16
    m = x2d.shape[0]
    lhs = mix_lhs(o_hgrn, z, o_att, hgrn_out_norm, attn_out_norm, _token_tile(m, 256))
    x1, h2 = mix_out(lhs, x2d, w_out16, norm_post_mix, norm_pre_ffn, _token_tile(m, 512), 512)
    return ffn(h2, x1, w_gate16, w_up16, w_down16, norm_post_ffn, _token_tile(m, 512), 256)


def kernel(x_prompt, x_sample, cache_k, cache_v, page_table, state_hgrn, norm_pre_mix, w_in,
           hgrn_lb_logits, hgrn_out_norm, attn_out_norm, w_out, norm_post_mix, norm_pre_ffn,
           w_gate, w_up, w_down, norm_post_ffn):
    bp, tp, d = x_prompt.shape
    bs, ts, _ = x_sample.shape
    assert bp == 1
    n_pool, page, h_att, hd = cache_k.shape
    past_len = page_table.shape[1] * page
    slopes = _alibi_slopes(N_ATT_HEADS)
    w_in16, w_out16, w_gate16, w_up16, w_down16 = (
        w.astype(BF16) for w in (w_in, w_out, w_gate, w_up, w_down))
    w16 = (w_out16, w_gate16, w_up16, w_down16)
    norms = (hgrn_out_norm, attn_out_norm, norm_post_mix, norm_pre_ffn, norm_post_ffn)

    xp = x_prompt.reshape(tp, d)
    zp, kp, vp = _project(xp, norm_pre_mix, w_in16)
    s0 = jnp.zeros((bp,) + state_hgrn.shape[1:], state_hgrn.dtype)
    oh_p, state_prompt = hgrn2(zp, hgrn_lb_logits, s0, HEAD_DIM, HEAD_DIM, 4)
    oa_p = moba_prompt(zp, kp, vp, slopes)
    y_prompt = _finish(xp, oh_p, zp, oa_p, w16, norms)

    xs = x_sample.reshape(bs * ts, d)
    zs, ks, vs = _project(xs, norm_pre_mix, w_in16)
    oh_s, state_sample = hgrn2(zs, hgrn_lb_logits, state_hgrn, ts, HGRN_SUB, 4)
    oa_s = moba_sample(zs, ks.reshape(bs * ts * h_att, hd), vs.reshape(bs * ts * h_att, hd),
                       cache_k.reshape(n_pool, page, 2, h_att // 2, hd),
                       cache_v.reshape(n_pool, page, 2, h_att // 2, hd), page_table, slopes, past_len)
    y_sample = _finish(xs, oh_s, zs, oa_s, w16, norms)

    return (y_prompt.reshape(bp, tp, d), y_sample.reshape(bs, ts, d),
            kp.reshape(bp, tp, h_att, hd), vp.reshape(bp, tp, h_att, hd),
            ks.reshape(bs, ts, h_att, hd), vs.reshape(bs, ts, h_att, hd),
            state_prompt, state_sample)
```

```python
import functools

import jax
import jax.numpy as jnp
from jax import lax
from jax.experimental import pallas as pl
from jax.experimental.pallas import tpu as pltpu

F32 = jnp.float32
BF16 = jnp.bfloat16

EPS = 1e-6
HEAD_DIM = 128
N_HGRN_HEADS = 16
N_ATT_HEADS = 16
W_GROUP = N_HGRN_HEADS * HEAD_DIM
MOBA_BLOCK = 256
MOBA_TOPK = 3
PAGE_SIZE = 128
PAGES_PER_BLOCK = MOBA_BLOCK // PAGE_SIZE
ATT_SCALE = HEAD_DIM ** -0.5
HGRN_SUB = 16
NEG_INF = float("-inf")
LOG2E = 1.4426950408889634
M_FLOOR = -1e30
MOBA_QB = 4
SAMPLE_BLOCKS_PER_STEP = 4
MOBA_GROUP = 8
VMEM_LIMIT = 56 * 1024 * 1024

ZQ, ZF, ZI, ZG, ZA = 0, 16, 32, 48, 64


def _cparams(sem):
    return pltpu.CompilerParams(dimension_semantics=sem, vmem_limit_bytes=VMEM_LIMIT)


def _nt(a, b, precision=None):
    return lax.dot_general(a, b, (((1,), (1,)), ((), ())), preferred_element_type=F32,
                           precision=precision)


def _sigmoid(x):
    return 1.0 / (1.0 + jnp.exp(-x))


def _norm_kernel(x_ref, g_ref, o_ref):
    x = x_ref[...]
    y = x * lax.rsqrt(jnp.mean(x * x, axis=-1, keepdims=True) + EPS)
    o_ref[...] = (y * g_ref[...]).astype(BF16)


def rmsnorm_bf16(x, g, tm):
    m, d = x.shape
    return pl.pallas_call(
        _norm_kernel,
        out_shape=jax.ShapeDtypeStruct((m, d), BF16),
        grid=(m // tm,),
        in_specs=[pl.BlockSpec((tm, d), lambda i: (i, 0)), pl.BlockSpec((1, d), lambda i: (0, 0))],
        out_specs=pl.BlockSpec((tm, d), lambda i: (i, 0)),
        compiler_params=_cparams(("arbitrary",)),
        name="rmsnorm_bf16",
    )(x, g.reshape(1, d))


def _mm_kernel(a_ref, w_ref, o_ref):
    o_ref[...] = jnp.dot(a_ref[...], w_ref[...], preferred_element_type=F32)


def matmul_cols(a, w, col0, n, tm, tn):
    m, k = a.shape
    cb = col0 // tn
    return pl.pallas_call(
        _mm_kernel,
        out_shape=jax.ShapeDtypeStruct((m, n), F32),
        grid=(m // tm, n // tn),
        in_specs=[pl.BlockSpec((tm, k), lambda i, j: (i, 0)),
                  pl.BlockSpec((k, tn), lambda i, j: (0, cb + j))],
        out_specs=pl.BlockSpec((tm, tn), lambda i, j: (i, j)),
        compiler_params=_cparams(("arbitrary", "arbitrary")),
        name="matmul_cols",
    )(a, w)


def _mm_cast_kernel(a_ref, w_ref, o_ref, w16_ref):
    w16 = w_ref[...].astype(BF16)
    w16_ref[...] = w16
    o_ref[...] = jnp.dot(a_ref[...], w16, preferred_element_type=F32)


def matmul_cols_cast(a, w, col0, n, tn):
    m, k = a.shape
    cb = col0 // tn
    return pl.pallas_call(
        _mm_cast_kernel,
        out_shape=(jax.ShapeDtypeStruct((m, n), F32), jax.ShapeDtypeStruct((k, n), BF16)),
        grid=(n // tn,),
        in_specs=[pl.BlockSpec((m, k), lambda j: (0, 0)),
                  pl.BlockSpec((k, tn), lambda j: (0, cb + j))],
        out_specs=(pl.BlockSpec((m, tn), lambda j: (0, j)), pl.BlockSpec((k, tn), lambda j: (0, j))),
        compiler_params=_cparams(("arbitrary",)),
        name="matmul_cols_cast",
    )(a, w)


def _cumsum_rows(g):
    c = g.shape[0]
    row = lax.broadcasted_iota(jnp.int32, g.shape, 0)
    b = g
    sh = 1
    while sh < c:
        b = b + jnp.where(row >= sh, pltpu.roll(b, sh, 0), 0.0)
        sh *= 2
    return b


def _hgrn_kernel(q_ref, f_ref, i_ref, lbl_ref, s0_ref, o_ref, sout_ref, st_ref, *, rows_in, rows, hp):
    c = pl.program_id(2)
    nc = pl.num_programs(2)
    for hh in range(hp):
        cols = slice(hh * HEAD_DIM, (hh + 1) * HEAD_DIM)

        @pl.when(c == 0)
        def _(hh=hh):
            st_ref[hh] = s0_ref[0, hh].T

        o, st_new = _hgrn_chunk(q_ref[:, cols], f_ref[:, cols], i_ref[:, cols], lbl_ref[:, cols],
                                st_ref[hh], rows_in, rows)
        o_ref[:, cols] = o
        st_ref[hh] = st_new

        @pl.when(c == nc - 1)
        def _(hh=hh, st_new=st_new):
            sout_ref[0, hh] = st_new.T


def _hgrn_chunk(hq, hf, v, lg, st, rows_in, rows):
    lge = jnp.exp(lg - jnp.max(lg, axis=0, keepdims=True))
    lb = lge[0:1, :] / jnp.sum(lge, axis=0, keepdims=True)

    f = lb + (1.0 - lb) * _sigmoid(hf)
    q = hq * _sigmoid(hq)
    k = 1.0 - f
    g = jnp.log(f)
    if rows > rows_in:
        pad = jnp.zeros((rows - rows_in, HEAD_DIM), F32)
        q, k, g, v = (jnp.concatenate([a, pad], axis=0) for a in (q, k, g, v))
    b = _cumsum_rows(g)

    m = HGRN_SUB
    ns = rows // m

    o = _nt((q * jnp.exp(b)).astype(BF16), st.astype(BF16))

    def exact_diag(width):
        row_w = lax.broadcasted_iota(jnp.int32, (width, HEAD_DIM), 0)
        outs = []
        for i in range(rows // width):
            sl = slice(i * width, (i + 1) * width)
            qi, ki, bi, vi = q[sl], k[sl], b[sl], v[sl]
            oi = jnp.zeros((width, HEAD_DIM), F32)
            for s in range(width):
                e = jnp.exp(jnp.where(row_w >= s, bi - bi[s:s + 1], NEG_INF))
                a = jnp.sum(qi * e * ki[s:s + 1], axis=-1, keepdims=True)
                oi = oi + a * vi[s:s + 1]
            outs.append(oi)
        return outs

    half = m // 2
    if ns >= 2:
        o_diag = exact_diag(half)
        q3, k3, b3, v3 = (a.reshape(ns, m, HEAD_DIM) for a in (q, k, b, v))
        r = b3[:, half - 1:half, :]
        qt = (q3[:, half:, :] * jnp.exp(b3[:, half:, :] - r)).reshape(ns * half, HEAD_DIM)
        kt = (k3[:, :half, :] * jnp.exp(r - b3[:, :half, :])).reshape(ns * half, HEAD_DIM)
        a = _nt(qt.astype(BF16), kt.astype(BF16))
        arow = lax.broadcasted_iota(jnp.int32, a.shape, 0) // half
        acol = lax.broadcasted_iota(jnp.int32, a.shape, 1) // half
        a = jnp.where(arow == acol, a, 0.0)
        low = jnp.dot(a.astype(BF16), v3[:, :half, :].reshape(ns * half, HEAD_DIM).astype(BF16),
                      preferred_element_type=F32)
        for i in range(ns):
            o_diag[2 * i + 1] = o_diag[2 * i + 1] + low[i * half:(i + 1) * half]
    else:
        o_diag = exact_diag(m)
    o = o + jnp.concatenate(o_diag, axis=0) if len(o_diag) > 1 else o + o_diag[0]

    for j in range(ns - 1):
        lo = (j + 1) * m
        r = b[lo - 1:lo]
        kt = k[j * m:lo] * jnp.exp(r - b[j * m:lo])
        qt = q[lo:] * jnp.exp(b[lo:] - r)
        a = _nt(qt.astype(BF16), kt.astype(BF16))
        contrib = jnp.dot(a.astype(BF16), v[j * m:lo].astype(BF16), preferred_element_type=F32)
        o = o + jnp.concatenate([jnp.zeros((lo, HEAD_DIM), F32), contrib], axis=0)

    b_last = b[rows - 1:rows]
    kdec = k * jnp.exp(b_last - b)
    vp, kp = v, kdec
    if rows < HEAD_DIM:
        zp = jnp.zeros((HEAD_DIM - rows, HEAD_DIM), F32)
        vp, kp = jnp.concatenate([v, zp], axis=0), jnp.concatenate([kdec, zp], axis=0)
    st_new = st * jnp.exp(b_last) + jnp.dot(vp.T.astype(BF16), kp.astype(BF16),
                                             preferred_element_type=F32)
    return o[:rows_in], st_new


def hgrn2(z, lb_logits, s0, rows_in, rows, hp):
    bsz = s0.shape[0]
    t = z.shape[0] // bsz
    nc = t // rows_in
    kern = functools.partial(_hgrn_kernel, rows_in=rows_in, rows=rows, hp=hp)
    wide = hp * HEAD_DIM

    def zspec(off):
        return pl.BlockSpec((rows_in, wide), lambda b, h, c: (b * nc + c, off // hp + h))

    sspec = pl.BlockSpec((1, hp, HEAD_DIM, HEAD_DIM), lambda b, h, c: (b, h, 0, 0))
    return pl.pallas_call(
        kern,
        out_shape=(jax.ShapeDtypeStruct((bsz * t, W_GROUP), F32),
                   jax.ShapeDtypeStruct(s0.shape, s0.dtype)),
        grid=(bsz, N_HGRN_HEADS // hp, nc),
        in_specs=[zspec(ZQ), zspec(ZF), zspec(ZI),
                  pl.BlockSpec((lb_logits.shape[0], wide), lambda b, h, c: (0, h)),
                  sspec],
        out_specs=(pl.BlockSpec((rows_in, wide), lambda b, h, c: (b * nc + c, h)), sspec),
        scratch_shapes=[pltpu.VMEM((hp, HEAD_DIM, HEAD_DIM), F32)],
        compiler_params=_cparams(("arbitrary", "arbitrary", "arbitrary")),
        name="hgrn2",
    )(z, z, z, lb_logits, s0)


def _topk_rows_mask(g, topk):
    nb = g.shape[0]
    jj = lax.broadcasted_iota(jnp.int32, g.shape, 0)
    sel = jnp.zeros(g.shape, F32)
    for _ in range(topk):
        mx = jnp.max(g, axis=0, keepdims=True)
        idx = jnp.min(jnp.where(g == mx, jj, nb), axis=0, keepdims=True)
        hit = jj == idx
        sel = jnp.where(jnp.logical_and(hit, mx > NEG_INF), 1.0, sel)
        g = jnp.where(hit, NEG_INF, g)
    return sel


def _split3_bf16(x):
    a = x.astype(BF16)
    r = x - a.astype(F32)
    b = r.astype(BF16)
    return a, b, (r - b.astype(F32)).astype(BF16)


def _moba_prompt_kernel(slope_ref, q_ref, k_ref, v_ref, o_ref,
                        ka_ref, vt_ref, kmean_ref, qa_ref, sel_ref, *, nb, topk, group):
    blk = MOBA_BLOCK
    qw = MOBA_QB * blk
    h = pl.program_id(0)
    qi = pl.program_id(1)
    slope2 = slope_ref[h] * LOG2E

    @pl.when(qi == 0)
    def _():
        lane = lax.broadcasted_iota(jnp.int32, (blk, HEAD_DIM), 1)
        rloc = lax.broadcasted_iota(jnp.int32, (blk, HEAD_DIM), 0).astype(F32)
        b1, b2, b3 = _split3_bf16(slope2 * rloc)
        aug = jnp.where(lane == 0, b1.astype(F32),
                        jnp.where(lane == 1, b2.astype(F32),
                                  jnp.where(lane == 2, b3.astype(F32), 0.0))).astype(BF16)

        def prep(j, carry):
            r0 = pl.multiple_of(j * blk, blk)
            kj = k_ref[pl.ds(r0, blk), :]
            ka_ref[pl.ds(r0, blk), :] = jnp.concatenate([kj.astype(BF16), aug], axis=1)
            vt_ref[:, pl.ds(r0, blk)] = v_ref[pl.ds(r0, blk), :].T.astype(BF16)
            rowj = lax.broadcasted_iota(jnp.int32, (nb, HEAD_DIM), 0)
            kmean_ref[...] = jnp.where(rowj == j, jnp.mean(kj, axis=0, keepdims=True), kmean_ref[...])
            return carry
        lax.fori_loop(0, nb, prep, 0)

    q = q_ref[...] * ATT_SCALE
    lane_q = lax.broadcasted_iota(jnp.int32, (qw, HEAD_DIM), 1)
    ones = jnp.where(lane_q < 3, 1.0, 0.0).astype(BF16)
    qa_ref[...] = jnp.concatenate([(q * LOG2E).astype(BF16), ones], axis=1)

    gate = _nt(kmean_ref[...], q, precision=lax.Precision.HIGHEST)
    jj = lax.broadcasted_iota(jnp.int32, (nb, qw), 0)
    qb_lane = MOBA_QB * qi + lax.broadcasted_iota(jnp.int32, (nb, qw), 1) // blk
    sel_ref[...] = _topk_rows_mask(jnp.where(jj < qb_lane, gate, NEG_INF), topk)

    qb_row = qb_lane[0:1, :]
    t_loc = lax.broadcasted_iota(jnp.int32, (blk, qw), 1) % blk
    r_loc = lax.broadcasted_iota(jnp.int32, (blk, qw), 0)
    causal_bias = jnp.where(t_loc >= r_loc, 0.0, NEG_INF)

    def body(g, carry, diag):
        m, l, acc = carry
        j0 = g * group
        c0 = pl.multiple_of(j0 * blk, group * blk)
        selg = sel_ref[pl.ds(pl.multiple_of(j0, group), group), :]
        s = _nt(ka_ref[pl.ds(c0, group * blk), :], qa_ref[...])
        ons, cjs, sbs = [], [], []
        mn = m
        for b in range(group):
            cj = slope2 * ((j0 + b) * blk).astype(F32)
            on = selg[b:b + 1, :] > 0.0
            sb = s[b * blk:(b + 1) * blk]
            if diag:
                own = qb_row == j0 + b
                sb = sb + jnp.where(own, causal_bias, jnp.where(on, 0.0, NEG_INF))
                on = jnp.where(own, 1.0, selg[b:b + 1, :]) > 0.0
            cm = jnp.max(sb, axis=0, keepdims=True) + cj
            mn = jnp.maximum(mn, jnp.where(on, cm, NEG_INF))
            ons.append(on)
            cjs.append(cj)
            sbs.append(sb)
        ps = [jnp.exp2(sbs[b] - jnp.where(ons[b], mn - cjs[b], float("inf"))) for b in range(group)]
        alpha = jnp.exp2(m - mn)
        lsum = ps[0].sum(axis=0, keepdims=True)
        for b in range(1, group):
            lsum = lsum + ps[b].sum(axis=0, keepdims=True)
        p = jnp.concatenate([x.astype(BF16) for x in ps], axis=0)
        pv = jnp.dot(vt_ref[:, pl.ds(c0, group * blk)], p, preferred_element_type=F32)
        return mn, l * alpha + lsum, acc * alpha + pv

    n_groups = (MOBA_QB * qi + MOBA_QB - 1 + group - 1) // group
    init = (jnp.full((1, qw), M_FLOOR, F32), jnp.zeros((1, qw), F32), jnp.zeros((HEAD_DIM, qw), F32))
    carry = lax.fori_loop(0, n_groups - 1, functools.partial(body, diag=False), init)
    _, l, acc = body(n_groups - 1, carry, True)
    o_ref[...] = (acc / l).T


def moba_prompt(z, k, v, slopes):
    t = k.shape[0]
    nb = t // MOBA_BLOCK
    topk = min(MOBA_TOPK, nb - 1)
    group = next(g for g in (MOBA_GROUP, 4, 2, 1) if nb % g == 0)
    qw = MOBA_QB * MOBA_BLOCK
    kern = functools.partial(_moba_prompt_kernel, nb=nb, topk=topk, group=group)
    grid_spec = pltpu.PrefetchScalarGridSpec(
        num_scalar_prefetch=1,
        grid=(N_ATT_HEADS, nb // MOBA_QB),
        in_specs=[pl.BlockSpec((qw, HEAD_DIM), lambda h, qi, sl: (qi, ZA + h)),
                  pl.BlockSpec((t, HEAD_DIM), lambda h, qi, sl: (0, h)),
                  pl.BlockSpec((t, HEAD_DIM), lambda h, qi, sl: (0, h))],
        out_specs=pl.BlockSpec((qw, HEAD_DIM), lambda h, qi, sl: (qi, h)),
        scratch_shapes=[pltpu.VMEM((t, 2 * HEAD_DIM), BF16),
                        pltpu.VMEM((HEAD_DIM, t), BF16),
                        pltpu.VMEM((nb, HEAD_DIM), F32),
                        pltpu.VMEM((qw, 2 * HEAD_DIM), BF16),
                        pltpu.VMEM((nb, qw), F32)],
    )
    return pl.pallas_call(
        kern,
        out_shape=jax.ShapeDtypeStruct((t, N_ATT_HEADS * HEAD_DIM), F32),
        grid_spec=grid_spec,
        compiler_params=_cparams(("arbitrary", "arbitrary")),
        name="moba_prompt",
    )(slopes, z, k, v)


def _lane_topk_mask(g, topk, n_valid):
    lane = lax.broadcasted_iota(jnp.int32, g.shape, 1).astype(F32)
    g = jnp.where(lane < n_valid, g, NEG_INF)
    sel = jnp.zeros(g.shape, F32)
    for _ in range(topk):
        mx = jnp.max(g, axis=1, keepdims=True)
        idx = jnp.min(jnp.where(g == mx, lane, float(g.shape[1])), axis=1, keepdims=True)
        hit = lane == idx
        sel = jnp.where(jnp.logical_and(hit, mx > NEG_INF), 1.0, sel)
        g = jnp.where(hit, NEG_INF, g)
    return sel


def _moba_sample_kernel(pt_ref, slope_ref, q_ref, kn_ref, vn_ref, *refs, nb, nt, past_len, bps):
    del pt_ref
    npg = bps * PAGES_PER_BLOCK
    ck_refs, cv_refs = refs[:npg], refs[npg:2 * npg]
    o_ref, qf_ref, q16_ref, bias_ref, ks_ref, m_ref, l_ref, acc_ref = refs[2 * npg:]
    blk = MOBA_BLOCK
    nh = N_ATT_HEADS
    hh = nh // 2
    nl = nh * nt
    nlh = hh * nt
    wide = blk * hh
    step = pl.program_id(1)
    slope2 = slope_ref[...] * LOG2E

    @pl.when(step == 0)
    def _():
        aq = q_ref[...] * ATT_SCALE
        qf = jnp.concatenate([aq[:, h * HEAD_DIM:(h + 1) * HEAD_DIM] for h in range(nh)], axis=0)
        qf_ref[...] = qf
        q16_ref[...] = (qf * LOG2E).astype(BF16)
        qrow = lax.broadcasted_iota(jnp.int32, (nl, wide), 0)
        kcol = lax.broadcasted_iota(jnp.int32, (nl, wide), 1)
        same_head = kcol % hh == (qrow // nt) % hh
        bias_ref[...] = jnp.where(same_head, -slope2 * (qrow % nt - kcol // hh).astype(F32), NEG_INF)
        m_ref[...] = jnp.zeros(m_ref.shape, F32)
        l_ref[...] = jnp.zeros(l_ref.shape, F32)

    lane = lax.broadcasted_iota(jnp.int32, (nl, HEAD_DIM), 1)
    m_all, l_all = m_ref[...], l_ref[...]
    for bb in range(bps):
        j = step * bps + bb
        pages = [PAGES_PER_BLOCK * bb + i for i in range(PAGES_PER_BLOCK)]
        mjs, ljs, accs = [], [], []
        for g in range(2):
            rows = slice(g * nlh, (g + 1) * nlh)
            kg = jnp.concatenate([ck_refs[i][0, :, g].reshape(PAGE_SIZE * hh, HEAD_DIM)
                                  for i in pages], axis=0)
            vg = jnp.concatenate([cv_refs[i][0, :, g].reshape(PAGE_SIZE * hh, HEAD_DIM)
                                  for i in pages], axis=0)
            ks_ref[pl.ds(pl.multiple_of(j * nh + g * hh, hh), hh), :] = jnp.sum(
                kg.reshape(blk, hh, HEAD_DIM), axis=0)
            s = _nt(q16_ref[rows, :], kg.astype(BF16)) + bias_ref[rows, :]
            mj = jnp.max(s, axis=1, keepdims=True)
            p = jnp.exp2(s - mj)
            mjs.append(mj)
            ljs.append(jnp.sum(p, axis=1, keepdims=True))
            accs.append(jnp.dot(p.astype(BF16), vg.astype(BF16), preferred_element_type=F32))
        acc_ref[j] = jnp.concatenate(accs, axis=0)
        off = (past_len - j * blk).astype(F32)
        m_all = jnp.where(lane == j, jnp.concatenate(mjs, axis=0) - slope2 * off, m_all)
        l_all = jnp.where(lane == j, jnp.concatenate(ljs, axis=0), l_all)
    m_ref[...] = m_all
    l_ref[...] = l_all

    @pl.when(step == nb // bps - 1)
    def _():
        qf = qf_ref[...]
        qrow = lax.broadcasted_iota(jnp.int32, (nl, nb * nh), 0)
        gcol = lax.broadcasted_iota(jnp.int32, (nl, nb * nh), 1)
        g_all = _nt(qf, ks_ref[...] * (1.0 / blk), precision=lax.Precision.HIGHEST)
        g_own = jnp.where(gcol % nh == qrow // nt, g_all, 0.0)
        erow = lax.broadcasted_iota(jnp.int32, (nb * nh, HEAD_DIM), 0)
        ecol = lax.broadcasted_iota(jnp.int32, (nb * nh, HEAD_DIM), 1)
        pick = jnp.where(erow // nh == ecol, 1.0, 0.0)
        gate = jnp.dot(g_own, pick, preferred_element_type=F32, precision=lax.Precision.HIGHEST)
        w = _lane_topk_mask(gate, min(MOBA_TOPK, nb), nb)

        orow = lax.broadcasted_iota(jnp.int32, (nl, nl), 0)
        ocol = lax.broadcasted_iota(jnp.int32, (nl, nl), 1)
        dist = orow % nt - ocol // nh
        ok = jnp.logical_and(ocol % nh == orow // nt, dist >= 0)
        so = jnp.where(ok, _nt(q16_ref[...], kn_ref[...].astype(BF16)) - slope2 * dist.astype(F32),
                       NEG_INF)
        mo = jnp.max(so, axis=1, keepdims=True)

        mall = m_ref[...]
        mtot = jnp.maximum(mo, jnp.max(jnp.where(w > 0.0, mall, NEG_INF), axis=1, keepdims=True))
        wj = jnp.where(w > 0.0, jnp.exp2(mall - mtot), 0.0)
        po = jnp.exp2(so - mtot)
        denom = jnp.sum(wj * l_ref[...], axis=1, keepdims=True) + jnp.sum(po, axis=1, keepdims=True)
        num = jnp.dot(po.astype(BF16), vn_ref[...].astype(BF16), preferred_element_type=F32)
        for jb in range(nb):
            num = num + wj[:, jb:jb + 1] * acc_ref[jb]
        out = num / denom
        for h in range(nh):
            o_ref[:, h * HEAD_DIM:(h + 1) * HEAD_DIM] = out[h * nt:(h + 1) * nt, :]


def moba_sample(z, k_new, v_new, cache_k, cache_v, page_table, slopes, past_len):
    bsz, n_pages = page_table.shape
    nh = N_ATT_HEADS
    nt = k_new.shape[0] // (bsz * nh)
    nb = past_len // MOBA_BLOCK
    assert n_pages == nb * PAGES_PER_BLOCK and PAGES_PER_BLOCK == 2
    assert nh * nt == HEAD_DIM and nb <= HEAD_DIM
    wd = nh * HEAD_DIM
    slope_col = jnp.repeat(slopes, nt).reshape(nh * nt, 1)
    bps = SAMPLE_BLOCKS_PER_STEP
    assert nb % bps == 0
    npg = bps * PAGES_PER_BLOCK
    kern = functools.partial(_moba_sample_kernel, nb=nb, nt=nt, past_len=past_len, bps=bps)

    def page_spec(which):
        return pl.BlockSpec((1, PAGE_SIZE, 2, nh // 2, HEAD_DIM),
                            lambda b, j, pt: (pt[b * n_pages + npg * j + which], 0, 0, 0, 0))

    new_spec = pl.BlockSpec((nt * nh, HEAD_DIM), lambda b, j, pt: (b, 0))
    grid_spec = pltpu.PrefetchScalarGridSpec(
        num_scalar_prefetch=1,
        grid=(bsz, nb // bps),
        in_specs=[pl.BlockSpec((nh * nt, 1), lambda b, j, pt: (0, 0)),
                  pl.BlockSpec((nt, wd), lambda b, j, pt: (b, ZA * HEAD_DIM // wd)),
                  new_spec, new_spec] + [page_spec(i) for i in range(npg)] * 2,
        out_specs=pl.BlockSpec((nt, wd), lambda b, j, pt: (b, 0)),
        scratch_shapes=[pltpu.VMEM((nh * nt, HEAD_DIM), F32),
                        pltpu.VMEM((nh * nt, HEAD_DIM), BF16),
                        pltpu.VMEM((nh * nt, MOBA_BLOCK * nh // 2), F32),
                        pltpu.VMEM((nb * nh, HEAD_DIM), F32),
                        pltpu.VMEM((nh * nt, HEAD_DIM), F32),
                        pltpu.VMEM((nh * nt, HEAD_DIM), F32),
                        pltpu.VMEM((nb, nh * nt, HEAD_DIM), F32)],
    )
    return pl.pallas_call(
        kern,
        out_shape=jax.ShapeDtypeStruct((bsz * nt, wd), F32),
        grid_spec=grid_spec,
        compiler_params=_cparams(("arbitrary", "arbitrary")),
        name="moba_sample",
    )(page_table.reshape(-1), slope_col, z, k_new, v_new, *([cache_k] * npg + [cache_v] * npg))


def _mix_lhs_kernel(oh_ref, g_ref, oa_ref, hn_ref, an_ref, o_ref):
    hn = hn_ref[...]
    for h in range(N_HGRN_HEADS):
        sl = slice(h * HEAD_DIM, (h + 1) * HEAD_DIM)
        o = oh_ref[:, sl]
        y = o * lax.rsqrt(jnp.mean(o * o, axis=-1, keepdims=True) + EPS) * hn
        gh = g_ref[:, sl]
        o_ref[:, sl] = (y * (gh * _sigmoid(gh))).astype(BF16)
    oa = oa_ref[...]
    ya = oa * lax.rsqrt(jnp.mean(oa * oa, axis=-1, keepdims=True) + EPS) * an_ref[...]
    o_ref[:, W_GROUP:] = ya.astype(BF16)


def mix_lhs(o_hgrn, z, o_att, hgrn_out_norm, attn_out_norm, tm):
    m = o_hgrn.shape[0]
    gsec = ZG * HEAD_DIM // W_GROUP
    return pl.pallas_call(
        _mix_lhs_kernel,
        out_shape=jax.ShapeDtypeStruct((m, 2 * W_GROUP), BF16),
        grid=(m // tm,),
        in_specs=[pl.BlockSpec((tm, W_GROUP), lambda i: (i, 0)),
                  pl.BlockSpec((tm, W_GROUP), lambda i: (i, gsec)),
                  pl.BlockSpec((tm, W_GROUP), lambda i: (i, 0)),
                  pl.BlockSpec((1, HEAD_DIM), lambda i: (0, 0)),
                  pl.BlockSpec((1, W_GROUP), lambda i: (0, 0))],
        out_specs=pl.BlockSpec((tm, 2 * W_GROUP), lambda i: (i, 0)),
        compiler_params=_cparams(("arbitrary",)),
        name="mix_lhs",
    )(o_hgrn, z, o_att, hgrn_out_norm.reshape(1, -1), attn_out_norm.reshape(1, -1))


def _mix_out_kernel(c_ref, x_ref, w_ref, pm_ref, pf_ref, x1_ref, h2_ref, *, tn):
    j = pl.program_id(1)
    nj = pl.num_programs(1)
    c0 = pl.multiple_of(j * tn, tn)
    x1_ref[:, pl.ds(c0, tn)] = jnp.dot(c_ref[...], w_ref[...], preferred_element_type=F32)

    @pl.when(j == nj - 1)
    def _():
        mix = x1_ref[...]
        x1 = x_ref[...] + mix * lax.rsqrt(jnp.mean(mix * mix, axis=-1, keepdims=True) + EPS) * pm_ref[...]
        x1_ref[...] = x1
        h2 = x1 * lax.rsqrt(jnp.mean(x1 * x1, axis=-1, keepdims=True) + EPS) * pf_ref[...]
        h2_ref[...] = h2.astype(BF16)


def mix_out(lhs, x, w_out, norm_post_mix, norm_pre_ffn, tm, tn):
    m, d = x.shape
    kern = functools.partial(_mix_out_kernel, tn=tn)
    vec = lambda n: pl.BlockSpec((1, n), lambda i, j: (0, 0))
    return pl.pallas_call(
        kern,
        out_shape=(jax.ShapeDtypeStruct((m, d), F32), jax.ShapeDtypeStruct((m, d), BF16)),
        grid=(m // tm, d // tn),
        in_specs=[pl.BlockSpec((tm, lhs.shape[1]), lambda i, j: (i, 0)),
                  pl.BlockSpec((tm, d), lambda i, j: (i, 0), pipeline_mode=pl.Buffered(1)),
                  pl.BlockSpec((lhs.shape[1], tn), lambda i, j: (0, j)),
                  vec(d), vec(d)],
        out_specs=(pl.BlockSpec((tm, d), lambda i, j: (i, 0)),
                   pl.BlockSpec((tm, d), lambda i, j: (i, 0))),
        compiler_params=_cparams(("arbitrary", "arbitrary")),
        name="mix_out",
    )(lhs, x, w_out, norm_post_mix.reshape(1, -1), norm_pre_ffn.reshape(1, -1))


def _ffn_kernel(h_ref, wg_ref, wu_ref, wd_ref, x1_ref, pn_ref, y_ref):
    f = pl.program_id(1)
    nf = pl.num_programs(1)

    @pl.when(f == 0)
    def _():
        y_ref[...] = jnp.zeros(y_ref.shape, F32)

    h = h_ref[...]
    g = jnp.dot(h, wg_ref[...], preferred_element_type=F32)
    u = jnp.dot(h, wu_ref[...], preferred_element_type=F32)
    a = (g * _sigmoid(g) * u).astype(BF16)
    y_ref[...] += jnp.dot(a, wd_ref[...], preferred_element_type=F32)

    @pl.when(f == nf - 1)
    def _():
        ff = y_ref[...]
        y_ref[...] = x1_ref[...] + ff * lax.rsqrt(jnp.mean(ff * ff, axis=-1, keepdims=True) + EPS) * pn_ref[...]


def ffn(h2, x1, w_gate, w_up, w_down, norm_post_ffn, tm, tf):
    m, d = h2.shape
    dff = w_gate.shape[1]
    once = pl.Buffered(1)
    return pl.pallas_call(
        _ffn_kernel,
        out_shape=jax.ShapeDtypeStruct((m, d), F32),
        grid=(m // tm, dff // tf),
        in_specs=[pl.BlockSpec((tm, d), lambda i, f: (i, 0), pipeline_mode=once),
                  pl.BlockSpec((d, tf), lambda i, f: (0, f)),
                  pl.BlockSpec((d, tf), lambda i, f: (0, f)),
                  pl.BlockSpec((tf, d), lambda i, f: (f, 0)),
                  pl.BlockSpec((tm, d), lambda i, f: (i, 0), pipeline_mode=once),
                  pl.BlockSpec((1, d), lambda i, f: (0, 0))],
        out_specs=pl.BlockSpec((tm, d), lambda i, f: (i, 0)),
        compiler_params=_cparams(("arbitrary", "arbitrary")),
        name="ffn",
    )(h2, w_gate, w_up, w_down, x1, norm_post_ffn.reshape(1, -1))


def _ffn_cast_kernel(h_ref, wg_ref, wu_ref, wd_ref, x1_ref, pn_ref, y_ref, wg16_ref, wu16_ref, wd16_ref):
    f = pl.program_id(0)
    nf = pl.num_programs(0)

    @pl.when(f == 0)
    def _():
        y_ref[...] = jnp.zeros(y_ref.shape, F32)

    wg, wu, wd = (r[...].astype(BF16) for r in (wg_ref, wu_ref, wd_ref))
    wg16_ref[...] = wg
    wu16_ref[...] = wu
    wd16_ref[...] = wd
    h = h_ref[...]
    g = jnp.dot(h, wg, preferred_element_type=F32)
    u = jnp.dot(h, wu, preferred_element_type=F32)
    a = (g * _sigmoid(g) * u).astype(BF16)
    y_ref[...] += jnp.dot(a, wd, preferred_element_type=F32)

    @pl.when(f == nf - 1)
    def _():
        ff = y_ref[...]
        y_ref[...] = x1_ref[...] + ff * lax.rsqrt(jnp.mean(ff * ff, axis=-1, keepdims=True) + EPS) * pn_ref[...]


def ffn_cast(h2, x1, w_gate, w_up, w_down, norm_post_ffn, tf):
    m, d = h2.shape
    dff = w_gate.shape[1]
    rows = pl.BlockSpec((m, d), lambda f: (0, 0))
    cols = pl.BlockSpec((d, tf), lambda f: (0, f))
    down = pl.BlockSpec((tf, d), lambda f: (f, 0))
    return pl.pallas_call(
        _ffn_cast_kernel,
        out_shape=(jax.ShapeDtypeStruct((m, d), F32), jax.ShapeDtypeStruct((d, dff), BF16),
                   jax.ShapeDtypeStruct((d, dff), BF16), jax.ShapeDtypeStruct((dff, d), BF16)),
        grid=(dff // tf,),
        in_specs=[rows, cols, cols, down, rows, pl.BlockSpec((1, d), lambda f: (0, 0))],
        out_specs=(rows, cols, cols, down),
        compiler_params=_cparams(("arbitrary",)),
        name="ffn_cast",
    )(h2, w_gate, w_up, w_down, x1, norm_post_ffn.reshape(1, -1))


def _alibi_slopes(n):
    return 2.0 ** (-8.0 * jnp.arange(1, n + 1, dtype=F32) / n)


def _token_tile(m, cap):
    return cap if m % cap == 0 else m


def _project(x2d, norm_pre_mix, w16_cols):
    m = x2d.shape[0]
    h = rmsnorm_bf16(x2d, norm_pre_mix, _token_tile(m, 256))
    tm = _token_tile(m, 1024)
    return tuple(matmul_cols(h, w, 0, w.shape[1], tm, 1024) for w in w16_cols)


def _project_cast(x2d, norm_pre_mix, w_in):
    h = rmsnorm_bf16(x2d, norm_pre_mix, x2d.shape[0])
    outs = [matmul_cols_cast(h, w_in, c0, n, 512)
            for c0, n in ((0, 5 * W_GROUP), (5 * W_GROUP, W_GROUP), (6 * W_GROUP, W_GROUP))]
    return tuple(o for o, _ in outs), tuple(w for _, w in outs)


def _mix(x2d, o_hgrn, z, o_att, w_out16, norms):
    hgrn_out_norm, attn_out_norm, norm_post_mix, norm_pre_ffn, _ = norms
    m = x2d.shape[0]
    lhs = mix_lhs(o_hgrn, z, o_att, hgrn_out_norm, attn_out_norm, _token_tile(m, 256))
    return mix_out(lhs, x2d, w_out16, norm_post_mix, norm_pre_ffn, _token_tile(m, 512), 512)


def kernel(x_prompt, x_sample, cache_k, cache_v, page_table, state_hgrn, norm_pre_mix, w_in,
           hgrn_lb_logits, hgrn_out_norm, attn_out_norm, w_out, norm_post_mix, norm_pre_ffn,
           w_gate, w_up, w_down, norm_post_ffn):
    bp, tp, d = x_prompt.shape
    bs, ts, _ = x_sample.shape
    assert bp == 1
    n_pool, page, h_att, hd = cache_k.shape
    past_len = page_table.shape[1] * page
    slopes = _alibi_slopes(N_ATT_HEADS)
    w_out16 = w_out.astype(BF16)
    norms = (hgrn_out_norm, attn_out_norm, norm_post_mix, norm_pre_ffn, norm_post_ffn)

    xs = x_sample.reshape(bs * ts, d)
    (zs, ks, vs), w_in16 = _project_cast(xs, norm_pre_mix, w_in)
    oh_s, state_sample = hgrn2(zs, hgrn_lb_logits, state_hgrn, ts, HGRN_SUB, 4)
    oa_s = moba_sample(zs, ks.reshape(bs * ts * h_att, hd), vs.reshape(bs * ts * h_att, hd),
                       cache_k.reshape(n_pool, page, 2, h_att // 2, hd),
                       cache_v.reshape(n_pool, page, 2, h_att // 2, hd), page_table, slopes, past_len)
    x1_s, h2_s = _mix(xs, oh_s, zs, oa_s, w_out16, norms)
    y_sample, w_gate16, w_up16, w_down16 = ffn_cast(h2_s, x1_s, w_gate, w_up, w_down, norm_post_ffn, 256)

    xp = x_prompt.reshape(tp, d)
    zp, kp, vp = _project(xp, norm_pre_mix, w_in16)
    s0 = jnp.zeros((bp,) + state_hgrn.shape[1:], state_hgrn.dtype)
    oh_p, state_prompt = hgrn2(zp, hgrn_lb_logits, s0, HEAD_DIM, HEAD_DIM, 4)
    oa_p = moba_prompt(zp, kp, vp, slopes)
    x1_p, h2_p = _mix(xp, oh_p, zp, oa_p, w_out16, norms)
    y_prompt = ffn(h2_p, x1_p, w_gate16, w_up16, w_down16, norm_post_ffn, _token_tile(tp, 512), 256)

    return (y_prompt.reshape(bp, tp, d), y_sample.reshape(bs, ts, d),
            kp.reshape(bp, tp, h_att, hd), vp.reshape(bp, tp, h_att, hd),
            ks.reshape(bs, ts, h_att, hd), vs.reshape(bs, ts, h_att, hd),
            state_prompt, state_sample)
```

```python
import functools

import jax
import jax.numpy as jnp
from jax import lax
from jax.experimental import pallas as pl
from jax.experimental.pallas import tpu as pltpu

F32 = jnp.float32
BF16 = jnp.bfloat16

EPS = 1e-6
HEAD_DIM = 128
N_HGRN_HEADS = 16
N_ATT_HEADS = 16
W_GROUP = N_HGRN_HEADS * HEAD_DIM
MOBA_BLOCK = 256
MOBA_TOPK = 3
PAGE_SIZE = 128
PAGES_PER_BLOCK = MOBA_BLOCK // PAGE_SIZE
ATT_SCALE = HEAD_DIM ** -0.5
HGRN_SUB = 16
NEG_INF = float("-inf")
LOG2E = 1.4426950408889634
M_FLOOR = -1e30
MOBA_QB = 4
SAMPLE_BLOCKS_PER_STEP = 4
MOBA_GROUP = 8
VMEM_LIMIT = 56 * 1024 * 1024

ZQ, ZF, ZI, ZG, ZA = 0, 16, 32, 48, 64


def _cparams(sem):
    return pltpu.CompilerParams(dimension_semantics=sem, vmem_limit_bytes=VMEM_LIMIT)


def _nt(a, b, precision=None):
    return lax.dot_general(a, b, (((1,), (1,)), ((), ())), preferred_element_type=F32,
                           precision=precision)


def _sigmoid(x):
    return 1.0 / (1.0 + jnp.exp(-x))


def _norm_kernel(x_ref, g_ref, o_ref):
    x = x_ref[...]
    y = x * lax.rsqrt(jnp.mean(x * x, axis=-1, keepdims=True) + EPS)
    o_ref[...] = (y * g_ref[...]).astype(BF16)


def rmsnorm_bf16(x, g, tm):
    m, d = x.shape
    return pl.pallas_call(
        _norm_kernel,
        out_shape=jax.ShapeDtypeStruct((m, d), BF16),
        grid=(m // tm,),
        in_specs=[pl.BlockSpec((tm, d), lambda i: (i, 0)), pl.BlockSpec((1, d), lambda i: (0, 0))],
        out_specs=pl.BlockSpec((tm, d), lambda i: (i, 0)),
        compiler_params=_cparams(("arbitrary",)),
        name="rmsnorm_bf16",
    )(x, g.reshape(1, d))


def _mm_kernel(a_ref, w_ref, o_ref):
    o_ref[...] = jnp.dot(a_ref[...], w_ref[...], preferred_element_type=F32)


def matmul_cols(a, w, col0, n, tm, tn):
    m, k = a.shape
    cb = col0 // tn
    return pl.pallas_call(
        _mm_kernel,
        out_shape=jax.ShapeDtypeStruct((m, n), F32),
        grid=(m // tm, n // tn),
        in_specs=[pl.BlockSpec((tm, k), lambda i, j: (i, 0)),
                  pl.BlockSpec((k, tn), lambda i, j: (0, cb + j))],
        out_specs=pl.BlockSpec((tm, tn), lambda i, j: (i, j)),
        compiler_params=_cparams(("arbitrary", "arbitrary")),
        name="matmul_cols",
    )(a, w)


def _mm_cast_kernel(a_ref, w_ref, o_ref, w16_ref):
    w16 = w_ref[...].astype(BF16)
    w16_ref[...] = w16
    o_ref[...] = jnp.dot(a_ref[...], w16, preferred_element_type=F32)


def matmul_cols_cast(a, w, col0, n, tn):
    m, k = a.shape
    cb = col0 // tn
    return pl.pallas_call(
        _mm_cast_kernel,
        out_shape=(jax.ShapeDtypeStruct((m, n), F32), jax.ShapeDtypeStruct((k, n), BF16)),
        grid=(n // tn,),
        in_specs=[pl.BlockSpec((m, k), lambda j: (0, 0)),
                  pl.BlockSpec((k, tn), lambda j: (0, cb + j))],
        out_specs=(pl.BlockSpec((m, tn), lambda j: (0, j)), pl.BlockSpec((k, tn), lambda j: (0, j))),
        compiler_params=_cparams(("arbitrary",)),
        name="matmul_cols_cast",
    )(a, w)


def _cumsum_rows(g):
    c = g.shape[0]
    row = lax.broadcasted_iota(jnp.int32, g.shape, 0)
    b = g
    sh = 1
    while sh < c:
        b = b + jnp.where(row >= sh, pltpu.roll(b, sh, 0), 0.0)
        sh *= 2
    return b


def _hgrn_kernel(q_ref, f_ref, i_ref, lbl_ref, s0_ref, o_ref, sout_ref, st_ref, *, rows_in, rows, hp):
    c = pl.program_id(2)
    nc = pl.num_programs(2)
    for hh in range(hp):
        cols = slice(hh * HEAD_DIM, (hh + 1) * HEAD_DIM)

        @pl.when(c == 0)
        def _(hh=hh):
            st_ref[hh] = s0_ref[0, hh].T

        o, st_new = _hgrn_chunk(q_ref[:, cols], f_ref[:, cols], i_ref[:, cols], lbl_ref[:, cols],
                                st_ref[hh], rows_in, rows)
        o_ref[:, cols] = o
        st_ref[hh] = st_new

        @pl.when(c == nc - 1)
        def _(hh=hh, st_new=st_new):
            sout_ref[0, hh] = st_new.T


def _hgrn_chunk(hq, hf, v, lg, st, rows_in, rows):
    lge = jnp.exp(lg - jnp.max(lg, axis=0, keepdims=True))
    lb = lge[0:1, :] / jnp.sum(lge, axis=0, keepdims=True)

    f = lb + (1.0 - lb) * _sigmoid(hf)
    q = hq * _sigmoid(hq)
    k = 1.0 - f
    g = jnp.log(f)
    if rows > rows_in:
        pad = jnp.zeros((rows - rows_in, HEAD_DIM), F32)
        q, k, g, v = (jnp.concatenate([a, pad], axis=0) for a in (q, k, g, v))
    b = _cumsum_rows(g)

    m = HGRN_SUB
    ns = rows // m

    o = _nt((q * jnp.exp(b)).astype(BF16), st.astype(BF16))

    def exact_diag(width):
        row_w = lax.broadcasted_iota(jnp.int32, (width, HEAD_DIM), 0)
        outs = []
        for i in range(rows // width):
            sl = slice(i * width, (i + 1) * width)
            qi, ki, bi, vi = q[sl], k[sl], b[sl], v[sl]
            oi = jnp.zeros((width, HEAD_DIM), F32)
            for s in range(width):
                e = jnp.exp(jnp.where(row_w >= s, bi - bi[s:s + 1], NEG_INF))
                a = jnp.sum(qi * e * ki[s:s + 1], axis=-1, keepdims=True)
                oi = oi + a * vi[s:s + 1]
            outs.append(oi)
        return outs

    half = m // 2
    if ns >= 2:
        o_diag = exact_diag(half)
        q3, k3, b3, v3 = (a.reshape(ns, m, HEAD_DIM) for a in (q, k, b, v))
        r = b3[:, half - 1:half, :]
        qt = (q3[:, half:, :] * jnp.exp(b3[:, half:, :] - r)).reshape(ns * half, HEAD_DIM)
        kt = (k3[:, :half, :] * jnp.exp(r - b3[:, :half, :])).reshape(ns * half, HEAD_DIM)
        a = _nt(qt.astype(BF16), kt.astype(BF16))
        arow = lax.broadcasted_iota(jnp.int32, a.shape, 0) // half
        acol = lax.broadcasted_iota(jnp.int32, a.shape, 1) // half
        a = jnp.where(arow == acol, a, 0.0)
        low = jnp.dot(a.astype(BF16), v3[:, :half, :].reshape(ns * half, HEAD_DIM).astype(BF16),
                      preferred_element_type=F32)
        for i in range(ns):
            o_diag[2 * i + 1] = o_diag[2 * i + 1] + low[i * half:(i + 1) * half]
    else:
        o_diag = exact_diag(m)
    o = o + jnp.concatenate(o_diag, axis=0) if len(o_diag) > 1 else o + o_diag[0]

    for j in range(ns - 1):
        lo = (j + 1) * m
        r = b[lo - 1:lo]
        kt = k[j * m:lo] * jnp.exp(r - b[j * m:lo])
        qt = q[lo:] * jnp.exp(b[lo:] - r)
        a = _nt(qt.astype(BF16), kt.astype(BF16))
        contrib = jnp.dot(a.astype(BF16), v[j * m:lo].astype(BF16), preferred_element_type=F32)
        o = o + jnp.concatenate([jnp.zeros((lo, HEAD_DIM), F32), contrib], axis=0)

    b_last = b[rows - 1:rows]
    kdec = k * jnp.exp(b_last - b)
    vp, kp = v, kdec
    if rows < HEAD_DIM:
        zp = jnp.zeros((HEAD_DIM - rows, HEAD_DIM), F32)
        vp, kp = jnp.concatenate([v, zp], axis=0), jnp.concatenate([kdec, zp], axis=0)
    st_new = st * jnp.exp(b_last) + jnp.dot(vp.T.astype(BF16), kp.astype(BF16),
                                             preferred_element_type=F32)
    return o[:rows_in], st_new


def hgrn2(z, lb_logits, s0, rows_in, rows, hp):
    bsz = s0.shape[0]
    t = z.shape[0] // bsz
    nc = t // rows_in
    kern = functools.partial(_hgrn_kernel, rows_in=rows_in, rows=rows, hp=hp)
    wide = hp * HEAD_DIM

    def zspec(off):
        return pl.BlockSpec((rows_in, wide), lambda b, h, c: (b * nc + c, off // hp + h))

    sspec = pl.BlockSpec((1, hp, HEAD_DIM, HEAD_DIM), lambda b, h, c: (b, h, 0, 0))
    return pl.pallas_call(
        kern,
        out_shape=(jax.ShapeDtypeStruct((bsz * t, W_GROUP), F32),
                   jax.ShapeDtypeStruct(s0.shape, s0.dtype)),
        grid=(bsz, N_HGRN_HEADS // hp, nc),
        in_specs=[zspec(ZQ), zspec(ZF), zspec(ZI),
                  pl.BlockSpec((lb_logits.shape[0], wide), lambda b, h, c: (0, h)),
                  sspec],
        out_specs=(pl.BlockSpec((rows_in, wide), lambda b, h, c: (b * nc + c, h)), sspec),
        scratch_shapes=[pltpu.VMEM((hp, HEAD_DIM, HEAD_DIM), F32)],
        compiler_params=_cparams(("arbitrary", "arbitrary", "arbitrary")),
        name="hgrn2",
    )(z, z, z, lb_logits, s0)


def _topk_rows_mask(g, topk):
    nb = g.shape[0]
    jj = lax.broadcasted_iota(jnp.int32, g.shape, 0)
    sel = jnp.zeros(g.shape, F32)
    for _ in range(topk):
        mx = jnp.max(g, axis=0, keepdims=True)
        idx = jnp.min(jnp.where(g == mx, jj, nb), axis=0, keepdims=True)
        hit = jj == idx
        sel = jnp.where(jnp.logical_and(hit, mx > NEG_INF), 1.0, sel)
        g = jnp.where(hit, NEG_INF, g)
    return sel


def _split3_bf16(x):
    a = x.astype(BF16)
    r = x - a.astype(F32)
    b = r.astype(BF16)
    return a, b, (r - b.astype(F32)).astype(BF16)


def _moba_prompt_kernel(slope_ref, q_ref, k_ref, v_ref, o_ref,
                        ka_ref, vt_ref, kmean_ref, qa_ref, sel_ref, *, nb, topk, group):
    blk = MOBA_BLOCK
    qw = MOBA_QB * blk
    h = pl.program_id(0)
    qi = pl.program_id(1)
    slope2 = slope_ref[h] * LOG2E

    @pl.when(qi == 0)
    def _():
        lane = lax.broadcasted_iota(jnp.int32, (blk, HEAD_DIM), 1)
        rloc = lax.broadcasted_iota(jnp.int32, (blk, HEAD_DIM), 0).astype(F32)
        b1, b2, b3 = _split3_bf16(slope2 * rloc)
        aug = jnp.where(lane == 0, b1.astype(F32),
                        jnp.where(lane == 1, b2.astype(F32),
                                  jnp.where(lane == 2, b3.astype(F32), 0.0))).astype(BF16)

        def prep(j, carry):
            r0 = pl.multiple_of(j * blk, blk)
            kj = k_ref[pl.ds(r0, blk), :]
            ka_ref[pl.ds(r0, blk), :] = jnp.concatenate([kj.astype(BF16), aug], axis=1)
            vt_ref[:, pl.ds(r0, blk)] = v_ref[pl.ds(r0, blk), :].T.astype(BF16)
            rowj = lax.broadcasted_iota(jnp.int32, (nb, HEAD_DIM), 0)
            kmean_ref[...] = jnp.where(rowj == j, jnp.mean(kj, axis=0, keepdims=True), kmean_ref[...])
            return carry
        lax.fori_loop(0, nb, prep, 0)

    q = q_ref[...] * ATT_SCALE
    lane_q = lax.broadcasted_iota(jnp.int32, (qw, HEAD_DIM), 1)
    ones = jnp.where(lane_q < 3, 1.0, 0.0).astype(BF16)
    qa_ref[...] = jnp.concatenate([(q * LOG2E).astype(BF16), ones], axis=1)

    gate = _nt(kmean_ref[...], q, precision=lax.Precision.HIGHEST)
    jj = lax.broadcasted_iota(jnp.int32, (nb, qw), 0)
    qb_lane = MOBA_QB * qi + lax.broadcasted_iota(jnp.int32, (nb, qw), 1) // blk
    sel_ref[...] = _topk_rows_mask(jnp.where(jj < qb_lane, gate, NEG_INF), topk)

    qb_row = qb_lane[0:1, :]
    t_loc = lax.broadcasted_iota(jnp.int32, (blk, qw), 1) % blk
    r_loc = lax.broadcasted_iota(jnp.int32, (blk, qw), 0)
    causal_bias = jnp.where(t_loc >= r_loc, 0.0, NEG_INF)

    def body(g, carry, diag, nblk=group):
        m, l, acc = carry
        j0 = g * group
        c0 = pl.multiple_of(j0 * blk, group * blk)
        selg = sel_ref[pl.ds(pl.multiple_of(j0, group), nblk), :]
        s = _nt(ka_ref[pl.ds(c0, nblk * blk), :], qa_ref[...])
        ons, cjs, sbs = [], [], []
        mn = m
        for b in range(nblk):
            cj = slope2 * ((j0 + b) * blk).astype(F32)
            on = selg[b:b + 1, :] > 0.0
            sb = s[b * blk:(b + 1) * blk]
            if diag:
                own = qb_row == j0 + b
                sb = sb + jnp.where(own, causal_bias, jnp.where(on, 0.0, NEG_INF))
                on = jnp.where(own, 1.0, selg[b:b + 1, :]) > 0.0
            cm = jnp.max(sb, axis=0, keepdims=True) + cj
            mn = jnp.maximum(mn, jnp.where(on, cm, NEG_INF))
            ons.append(on)
            cjs.append(cj)
            sbs.append(sb)
        ps = [jnp.exp2(sbs[b] - jnp.where(ons[b], mn - cjs[b], float("inf"))) for b in range(nblk)]
        alpha = jnp.exp2(m - mn)
        lsum = ps[0].sum(axis=0, keepdims=True)
        for b in range(1, nblk):
            lsum = lsum + ps[b].sum(axis=0, keepdims=True)
        p = jnp.concatenate([x.astype(BF16) for x in ps], axis=0)
        pv = jnp.dot(vt_ref[:, pl.ds(c0, nblk * blk)], p, preferred_element_type=F32)
        return mn, l * alpha + lsum, acc * alpha + pv

    n_groups = (MOBA_QB * qi + MOBA_QB - 1 + group - 1) // group
    init = (jnp.full((1, qw), M_FLOOR, F32), jnp.zeros((1, qw), F32), jnp.zeros((HEAD_DIM, qw), F32))
    carry = lax.fori_loop(0, n_groups - 1, functools.partial(body, diag=False), init)
    for offset in range(0, group, MOBA_QB):
        @pl.when((MOBA_QB * qi) % group == offset)
        def _(offset=offset):
            _, l, acc = body(n_groups - 1, carry, True, offset + MOBA_QB)
            o_ref[...] = (acc / l).T


def moba_prompt(z, k, v, slopes):
    t = k.shape[0]
    nb = t // MOBA_BLOCK
    topk = min(MOBA_TOPK, nb - 1)
    group = next(g for g in (MOBA_GROUP, 4, 2, 1) if nb % g == 0)
    qw = MOBA_QB * MOBA_BLOCK
    kern = functools.partial(_moba_prompt_kernel, nb=nb, topk=topk, group=group)
    grid_spec = pltpu.PrefetchScalarGridSpec(
        num_scalar_prefetch=1,
        grid=(N_ATT_HEADS, nb // MOBA_QB),
        in_specs=[pl.BlockSpec((qw, HEAD_DIM), lambda h, qi, sl: (qi, ZA + h)),
                  pl.BlockSpec((t, HEAD_DIM), lambda h, qi, sl: (0, h)),
                  pl.BlockSpec((t, HEAD_DIM), lambda h, qi, sl: (0, h))],
        out_specs=pl.BlockSpec((qw, HEAD_DIM), lambda h, qi, sl: (qi, h)),
        scratch_shapes=[pltpu.VMEM((t, 2 * HEAD_DIM), BF16),
                        pltpu.VMEM((HEAD_DIM, t), BF16),
                        pltpu.VMEM((nb, HEAD_DIM), F32),
                        pltpu.VMEM((qw, 2 * HEAD_DIM), BF16),
                        pltpu.VMEM((nb, qw), F32)],
    )
    return pl.pallas_call(
        kern,
        out_shape=jax.ShapeDtypeStruct((t, N_ATT_HEADS * HEAD_DIM), F32),
        grid_spec=grid_spec,
        compiler_params=_cparams(("arbitrary", "arbitrary")),
        name="moba_prompt",
    )(slopes, z, k, v)


def _lane_topk_mask(g, topk, n_valid):
    lane = lax.broadcasted_iota(jnp.int32, g.shape, 1).astype(F32)
    g = jnp.where(lane < n_valid, g, NEG_INF)
    sel = jnp.zeros(g.shape, F32)
    for _ in range(topk):
        mx = jnp.max(g, axis=1, keepdims=True)
        idx = jnp.min(jnp.where(g == mx, lane, float(g.shape[1])), axis=1, keepdims=True)
        hit = lane == idx
        sel = jnp.where(jnp.logical_and(hit, mx > NEG_INF), 1.0, sel)
        g = jnp.where(hit, NEG_INF, g)
    return sel


def _moba_sample_kernel(pt_ref, slope_ref, q_ref, kn_ref, vn_ref, *refs, nb, nt, past_len, bps):
    del pt_ref
    npg = bps * PAGES_PER_BLOCK
    ck_refs, cv_refs = refs[:npg], refs[npg:2 * npg]
    o_ref, qf_ref, q16_ref, bias_ref, ks_ref, m_ref, l_ref, acc_ref = refs[2 * npg:]
    blk = MOBA_BLOCK
    nh = N_ATT_HEADS
    hh = nh // 2
    nl = nh * nt
    nlh = hh * nt
    wide = blk * hh
    step = pl.program_id(1)
    slope2 = slope_ref[...] * LOG2E

    @pl.when(step == 0)
    def _():
        aq = q_ref[...] * ATT_SCALE
        qf = jnp.concatenate([aq[:, h * HEAD_DIM:(h + 1) * HEAD_DIM] for h in range(nh)], axis=0)
        qf_ref[...] = qf
        q16_ref[...] = (qf * LOG2E).astype(BF16)
        qrow = lax.broadcasted_iota(jnp.int32, (nl, wide), 0)
        kcol = lax.broadcasted_iota(jnp.int32, (nl, wide), 1)
        same_head = kcol % hh == (qrow // nt) % hh
        bias_ref[...] = jnp.where(same_head, -slope2 * (qrow % nt - kcol // hh).astype(F32), NEG_INF)
        m_ref[...] = jnp.zeros(m_ref.shape, F32)
        l_ref[...] = jnp.zeros(l_ref.shape, F32)

    lane = lax.broadcasted_iota(jnp.int32, (nl, HEAD_DIM), 1)
    m_all, l_all = m_ref[...], l_ref[...]
    for bb in range(bps):
        j = step * bps + bb
        pages = [PAGES_PER_BLOCK * bb + i for i in range(PAGES_PER_BLOCK)]
        mjs, ljs, accs = [], [], []
        for g in range(2):
            rows = slice(g * nlh, (g + 1) * nlh)
            kg = jnp.concatenate([ck_refs[i][0, :, g].reshape(PAGE_SIZE * hh, HEAD_DIM)
                                  for i in pages], axis=0)
            vg = jnp.concatenate([cv_refs[i][0, :, g].reshape(PAGE_SIZE * hh, HEAD_DIM)
                                  for i in pages], axis=0)
            ks_ref[pl.ds(pl.multiple_of(j * nh + g * hh, hh), hh), :] = jnp.sum(
                kg.reshape(blk, hh, HEAD_DIM), axis=0)
            s = _nt(q16_ref[rows, :], kg.astype(BF16)) + bias_ref[rows, :]
            mj = jnp.max(s, axis=1, keepdims=True)
            p = jnp.exp2(s - mj)
            mjs.append(mj)
            ljs.append(jnp.sum(p, axis=1, keepdims=True))
            accs.append(jnp.dot(p.astype(BF16), vg.astype(BF16), preferred_element_type=F32))
        acc_ref[j] = jnp.concatenate(accs, axis=0)
        off = (past_len - j * blk).astype(F32)
        m_all = jnp.where(lane == j, jnp.concatenate(mjs, axis=0) - slope2 * off, m_all)
        l_all = jnp.where(lane == j, jnp.concatenate(ljs, axis=0), l_all)
    m_ref[...] = m_all
    l_ref[...] = l_all

    @pl.when(step == nb // bps - 1)
    def _():
        qf = qf_ref[...]
        qrow = lax.broadcasted_iota(jnp.int32, (nl, nb * nh), 0)
        gcol = lax.broadcasted_iota(jnp.int32, (nl, nb * nh), 1)
        g_all = _nt(qf, ks_ref[...] * (1.0 / blk), precision=lax.Precision.HIGHEST)
        g_own = jnp.where(gcol % nh == qrow // nt, g_all, 0.0)
        erow = lax.broadcasted_iota(jnp.int32, (nb * nh, HEAD_DIM), 0)
        ecol = lax.broadcasted_iota(jnp.int32, (nb * nh, HEAD_DIM), 1)
        pick = jnp.where(erow // nh == ecol, 1.0, 0.0)
        gate = jnp.dot(g_own, pick, preferred_element_type=F32, precision=lax.Precision.HIGHEST)
        w = _lane_topk_mask(gate, min(MOBA_TOPK, nb), nb)

        orow = lax.broadcasted_iota(jnp.int32, (nl, nl), 0)
        ocol = lax.broadcasted_iota(jnp.int32, (nl, nl), 1)
        dist = orow % nt - ocol // nh
        ok = jnp.logical_and(ocol % nh == orow // nt, dist >= 0)
        so = jnp.where(ok, _nt(q16_ref[...], kn_ref[...].astype(BF16)) - slope2 * dist.astype(F32),
                       NEG_INF)
        mo = jnp.max(so, axis=1, keepdims=True)

        mall = m_ref[...]
        mtot = jnp.maximum(mo, jnp.max(jnp.where(w > 0.0, mall, NEG_INF), axis=1, keepdims=True))
        wj = jnp.where(w > 0.0, jnp.exp2(mall - mtot), 0.0)
        po = jnp.exp2(so - mtot)
        denom = jnp.sum(wj * l_ref[...], axis=1, keepdims=True) + jnp.sum(po, axis=1, keepdims=True)
        num = jnp.dot(po.astype(BF16), vn_ref[...].astype(BF16), preferred_element_type=F32)
        for jb in range(nb):
            num = num + wj[:, jb:jb + 1] * acc_ref[jb]
        out = num / denom
        for h in range(nh):
            o_ref[:, h * HEAD_DIM:(h + 1) * HEAD_DIM] = out[h * nt:(h + 1) * nt, :]


def moba_sample(z, k_new, v_new, cache_k, cache_v, page_table, slopes, past_len):
    bsz, n_pages = page_table.shape
    nh = N_ATT_HEADS
    nt = k_new.shape[0] // (bsz * nh)
    nb = past_len // MOBA_BLOCK
    assert n_pages == nb * PAGES_PER_BLOCK and PAGES_PER_BLOCK == 2
    assert nh * nt == HEAD_DIM and nb <= HEAD_DIM
    wd = nh * HEAD_DIM
    slope_col = jnp.repeat(slopes, nt).reshape(nh * nt, 1)
    bps = SAMPLE_BLOCKS_PER_STEP
    assert nb % bps == 0
    npg = bps * PAGES_PER_BLOCK
    kern = functools.partial(_moba_sample_kernel, nb=nb, nt=nt, past_len=past_len, bps=bps)

    def page_spec(which):
        return pl.BlockSpec((1, PAGE_SIZE, 2, nh // 2, HEAD_DIM),
                            lambda b, j, pt: (pt[b * n_pages + npg * j + which], 0, 0, 0, 0))

    new_spec = pl.BlockSpec((nt * nh, HEAD_DIM), lambda b, j, pt: (b, 0))
    grid_spec = pltpu.PrefetchScalarGridSpec(
        num_scalar_prefetch=1,
        grid=(bsz, nb // bps),
        in_specs=[pl.BlockSpec((nh * nt, 1), lambda b, j, pt: (0, 0)),
                  pl.BlockSpec((nt, wd), lambda b, j, pt: (b, ZA * HEAD_DIM // wd)),
                  new_spec, new_spec] + [page_spec(i) for i in range(npg)] * 2,
        out_specs=pl.BlockSpec((nt, wd), lambda b, j, pt: (b, 0)),
        scratch_shapes=[pltpu.VMEM((nh * nt, HEAD_DIM), F32),
                        pltpu.VMEM((nh * nt, HEAD_DIM), BF16),
                        pltpu.VMEM((nh * nt, MOBA_BLOCK * nh // 2), F32),
                        pltpu.VMEM((nb * nh, HEAD_DIM), F32),
                        pltpu.VMEM((nh * nt, HEAD_DIM), F32),
                        pltpu.VMEM((nh * nt, HEAD_DIM), F32),
                        pltpu.VMEM((nb, nh * nt, HEAD_DIM), F32)],
    )
    return pl.pallas_call(
        kern,
        out_shape=jax.ShapeDtypeStruct((bsz * nt, wd), F32),
        grid_spec=grid_spec,
        compiler_params=_cparams(("arbitrary", "arbitrary")),
        name="moba_sample",
    )(page_table.reshape(-1), slope_col, z, k_new, v_new, *([cache_k] * npg + [cache_v] * npg))


def _mix_lhs_kernel(oh_ref, g_ref, oa_ref, hn_ref, an_ref, o_ref):
    hn = hn_ref[...]
    for h in range(N_HGRN_HEADS):
        sl = slice(h * HEAD_DIM, (h + 1) * HEAD_DIM)
        o = oh_ref[:, sl]
        y = o * lax.rsqrt(jnp.mean(o * o, axis=-1, keepdims=True) + EPS) * hn
        gh = g_ref[:, sl]
        o_ref[:, sl] = (y * (gh * _sigmoid(gh))).astype(BF16)
    oa = oa_ref[...]
    ya = oa * lax.rsqrt(jnp.mean(oa * oa, axis=-1, keepdims=True) + EPS) * an_ref[...]
    o_ref[:, W_GROUP:] = ya.astype(BF16)


def mix_lhs(o_hgrn, z, o_att, hgrn_out_norm, attn_out_norm, tm):
    m = o_hgrn.shape[0]
    gsec = ZG * HEAD_DIM // W_GROUP
    return pl.pallas_call(
        _mix_lhs_kernel,
        out_shape=jax.ShapeDtypeStruct((m, 2 * W_GROUP), BF16),
        grid=(m // tm,),
        in_specs=[pl.BlockSpec((tm, W_GROUP), lambda i: (i, 0)),
                  pl.BlockSpec((tm, W_GROUP), lambda i: (i, gsec)),
                  pl.BlockSpec((tm, W_GROUP), lambda i: (i, 0)),
                  pl.BlockSpec((1, HEAD_DIM), lambda i: (0, 0)),
                  pl.BlockSpec((1, W_GROUP), lambda i: (0, 0))],
        out_specs=pl.BlockSpec((tm, 2 * W_GROUP), lambda i: (i, 0)),
        compiler_params=_cparams(("arbitrary",)),
        name="mix_lhs",
    )(o_hgrn, z, o_att, hgrn_out_norm.reshape(1, -1), attn_out_norm.reshape(1, -1))


def _mix_out_kernel(c_ref, x_ref, w_ref, pm_ref, pf_ref, x1_ref, h2_ref, *, tn):
    j = pl.program_id(1)
    nj = pl.num_programs(1)
    c0 = pl.multiple_of(j * tn, tn)
    x1_ref[:, pl.ds(c0, tn)] = jnp.dot(c_ref[...], w_ref[...], preferred_element_type=F32)

    @pl.when(j == nj - 1)
    def _():
        mix = x1_ref[...]
        x1 = x_ref[...] + mix * lax.rsqrt(jnp.mean(mix * mix, axis=-1, keepdims=True) + EPS) * pm_ref[...]
        x1_ref[...] = x1
        h2 = x1 * lax.rsqrt(jnp.mean(x1 * x1, axis=-1, keepdims=True) + EPS) * pf_ref[...]
        h2_ref[...] = h2.astype(BF16)


def mix_out(lhs, x, w_out, norm_post_mix, norm_pre_ffn, tm, tn):
    m, d = x.shape
    kern = functools.partial(_mix_out_kernel, tn=tn)
    vec = lambda n: pl.BlockSpec((1, n), lambda i, j: (0, 0))
    return pl.pallas_call(
        kern,
        out_shape=(jax.ShapeDtypeStruct((m, d), F32), jax.ShapeDtypeStruct((m, d), BF16)),
        grid=(m // tm, d // tn),
        in_specs=[pl.BlockSpec((tm, lhs.shape[1]), lambda i, j: (i, 0)),
                  pl.BlockSpec((tm, d), lambda i, j: (i, 0), pipeline_mode=pl.Buffered(1)),
                  pl.BlockSpec((lhs.shape[1], tn), lambda i, j: (0, j)),
                  vec(d), vec(d)],
        out_specs=(pl.BlockSpec((tm, d), lambda i, j: (i, 0)),
                   pl.BlockSpec((tm, d), lambda i, j: (i, 0))),
        compiler_params=_cparams(("arbitrary", "arbitrary")),
        name="mix_out",
    )(lhs, x, w_out, norm_post_mix.reshape(1, -1), norm_pre_ffn.reshape(1, -1))


def _ffn_kernel(h_ref, wg_ref, wu_ref, wd_ref, x1_ref, pn_ref, y_ref):
    f = pl.program_id(1)
    nf = pl.num_programs(1)

    @pl.when(f == 0)
    def _():
        y_ref[...] = jnp.zeros(y_ref.shape, F32)

    h = h_ref[...]
    g = jnp.dot(h, wg_ref[...], preferred_element_type=F32)
    u = jnp.dot(h, wu_ref[...], preferred_element_type=F32)
    a = (g * _sigmoid(g) * u).astype(BF16)
    y_ref[...] += jnp.dot(a, wd_ref[...], preferred_element_type=F32)

    @pl.when(f == nf - 1)
    def _():
        ff = y_ref[...]
        y_ref[...] = x1_ref[...] + ff * lax.rsqrt(jnp.mean(ff * ff, axis=-1, keepdims=True) + EPS) * pn_ref[...]


def ffn(h2, x1, w_gate, w_up, w_down, norm_post_ffn, tm, tf):
    m, d = h2.shape
    dff = w_gate.shape[1]
    once = pl.Buffered(1)
    return pl.pallas_call(
        _ffn_kernel,
        out_shape=jax.ShapeDtypeStruct((m, d), F32),
        grid=(m // tm, dff // tf),
        in_specs=[pl.BlockSpec((tm, d), lambda i, f: (i, 0), pipeline_mode=once),
                  pl.BlockSpec((d, tf), lambda i, f: (0, f)),
                  pl.BlockSpec((d, tf), lambda i, f: (0, f)),
                  pl.BlockSpec((tf, d), lambda i, f: (f, 0)),
                  pl.BlockSpec((tm, d), lambda i, f: (i, 0), pipeline_mode=once),
                  pl.BlockSpec((1, d), lambda i, f: (0, 0))],
        out_specs=pl.BlockSpec((tm, d), lambda i, f: (i, 0)),
        compiler_params=_cparams(("arbitrary", "arbitrary")),
        name="ffn",
    )(h2, w_gate, w_up, w_down, x1, norm_post_ffn.reshape(1, -1))


def _ffn_cast_kernel(h_ref, wg_ref, wu_ref, wd_ref, x1_ref, pn_ref, y_ref, wg16_ref, wu16_ref, wd16_ref):
    f = pl.program_id(0)
    nf = pl.num_programs(0)

    @pl.when(f == 0)
    def _():
        y_ref[...] = jnp.zeros(y_ref.shape, F32)

    wg, wu, wd = (r[...].astype(BF16) for r in (wg_ref, wu_ref, wd_ref))
    wg16_ref[...] = wg
    wu16_ref[...] = wu
    wd16_ref[...] = wd
    h = h_ref[...]
    g = jnp.dot(h, wg, preferred_element_type=F32)
    u = jnp.dot(h, wu, preferred_element_type=F32)
    a = (g * _sigmoid(g) * u).astype(BF16)
    y_ref[...] += jnp.dot(a, wd, preferred_element_type=F32)

    @pl.when(f == nf - 1)
    def _():
        ff = y_ref[...]
        y_ref[...] = x1_ref[...] + ff * lax.rsqrt(jnp.mean(ff * ff, axis=-1, keepdims=True) + EPS) * pn_ref[...]


def ffn_cast(h2, x1, w_gate, w_up, w_down, norm_post_ffn, tf):
    m, d = h2.shape
    dff = w_gate.shape[1]
    rows = pl.BlockSpec((m, d), lambda f: (0, 0))
    cols = pl.BlockSpec((d, tf), lambda f: (0, f))
    down = pl.BlockSpec((tf, d), lambda f: (f, 0))
    return pl.pallas_call(
        _ffn_cast_kernel,
        out_shape=(jax.ShapeDtypeStruct((m, d), F32), jax.ShapeDtypeStruct((d, dff), BF16),
                   jax.ShapeDtypeStruct((d, dff), BF16), jax.ShapeDtypeStruct((dff, d), BF16)),
        grid=(dff // tf,),
        in_specs=[rows, cols, cols, down, rows, pl.BlockSpec((1, d), lambda f: (0, 0))],
        out_specs=(rows, cols, cols, down),
        compiler_params=_cparams(("arbitrary",)),
        name="ffn_cast",
    )(h2, w_gate, w_up, w_down, x1, norm_post_ffn.reshape(1, -1))


def _alibi_slopes(n):
    return 2.0 ** (-8.0 * jnp.arange(1, n + 1, dtype=F32) / n)


def _token_tile(m, cap):
    return cap if m % cap == 0 else m


def _project(x2d, norm_pre_mix, w16_cols):
    m = x2d.shape[0]
    h = rmsnorm_bf16(x2d, norm_pre_mix, _token_tile(m, 256))
    tm = _token_tile(m, 1024)
    return tuple(matmul_cols(h, w, 0, w.shape[1], tm, 1024) for w in w16_cols)


def _project_cast(x2d, norm_pre_mix, w_in):
    h = rmsnorm_bf16(x2d, norm_pre_mix, x2d.shape[0])
    outs = [matmul_cols_cast(h, w_in, c0, n, 512)
            for c0, n in ((0, 5 * W_GROUP), (5 * W_GROUP, W_GROUP), (6 * W_GROUP, W_GROUP))]
    return tuple(o for o, _ in outs), tuple(w for _, w in outs)


def _mix(x2d, o_hgrn, z, o_att, w_out16, norms):
    hgrn_out_norm, attn_out_norm, norm_post_mix, norm_pre_ffn, _ = norms
    m = x2d.shape[0]
    lhs = mix_lhs(o_hgrn, z, o_att, hgrn_out_norm, attn_out_norm, _token_tile(m, 256))
    return mix_out(lhs, x2d, w_out16, norm_post_mix, norm_pre_ffn, _token_tile(m, 512), 512)


def kernel(x_prompt, x_sample, cache_k, cache_v, page_table, state_hgrn, norm_pre_mix, w_in,
           hgrn_lb_logits, hgrn_out_norm, attn_out_norm, w_out, norm_post_mix, norm_pre_ffn,
           w_gate, w_up, w_down, norm_post_ffn):
    bp, tp, d = x_prompt.shape
    bs, ts, _ = x_sample.shape
    assert bp == 1
    n_pool, page, h_att, hd = cache_k.shape
    past_len = page_table.shape[1] * page
    slopes = _alibi_slopes(N_ATT_HEADS)
    w_out16 = w_out.astype(BF16)
    norms = (hgrn_out_norm, attn_out_norm, norm_post_mix, norm_pre_ffn, norm_post_ffn)

    xs = x_sample.reshape(bs * ts, d)
    (zs, ks, vs), w_in16 = _project_cast(xs, norm_pre_mix, w_in)
    oh_s, state_sample = hgrn2(zs, hgrn_lb_logits, state_hgrn, ts, HGRN_SUB, 4)
    oa_s = moba_sample(zs, ks.reshape(bs * ts * h_att, hd), vs.reshape(bs * ts * h_att, hd),
                       cache_k.reshape(n_pool, page, 2, h_att // 2, hd),
                       cache_v.reshape(n_pool, page, 2, h_att // 2, hd), page_table, slopes, past_len)
    x1_s, h2_s = _mix(xs, oh_s, zs, oa_s, w_out16, norms)
    y_sample, w_gate16, w_up16, w_down16 = ffn_cast(h2_s, x1_s, w_gate, w_up, w_down, norm_post_ffn, 256)

    xp = x_prompt.reshape(tp, d)
    zp, kp, vp = _project(xp, norm_pre_mix, w_in16)
    s0 = jnp.zeros((bp,) + state_hgrn.shape[1:], state_hgrn.dtype)
    oh_p, state_prompt = hgrn2(zp, hgrn_lb_logits, s0, HEAD_DIM, HEAD_DIM, 4)
    oa_p = moba_prompt(zp, kp, vp, slopes)
    x1_p, h2_p = _mix(xp, oh_p, zp, oa_p, w_out16, norms)
    y_prompt = ffn(h2_p, x1_p, w_gate16, w_up16, w_down16, norm_post_ffn, _token_tile(tp, 512), 256)

    return (y_prompt.reshape(bp, tp, d), y_sample.reshape(bs, ts, d),
            kp.reshape(bp, tp, h_att, hd), vp.reshape(bp, tp, h_att, hd),
            ks.reshape(bs, ts, h_att, hd), vs.reshape(bs, ts, h_att, hd),
            state_prompt, state_sample)
```
